```python
import jax, jax.numpy as jnp
from jax import lax
import numpy as np

D_MODEL = 1024
BATCH = 16
SEQ = 256
DEPTH = 1
DEC_BATCH = 8
DEC_SEQ = 4096
PAST_LEN = 256

GRID_W = 64
GLA_HEADS = 4
GLA_DK = D_MODEL // 16
GLA_DV = D_MODEL // 8
GLA_WIDTH = GLA_HEADS * GLA_DV
GATE_RANK = 16
GATE_NORM = 16.0
CHUNK = 64
POOL_WINDOWS = (2, 4, 8, 16)
POOL_GROUPS = 4
POOL_WIDTH = D_MODEL // 2
POOL_GC = POOL_WIDTH // POOL_GROUPS
MIX_WIDTH = GLA_WIDTH + POOL_WIDTH
N_EXPERTS = 16
CAPACITY_FACTOR = 2
D_FF_EXPERT = ((8 * D_MODEL // 3 + 127) // 128) * 128
SPLITS = (GLA_HEADS * GLA_DK, GLA_HEADS * GLA_DK, GLA_WIDTH, GLA_WIDTH, GATE_RANK, GATE_RANK, POOL_WIDTH)
IN_COLS = sum(SPLITS)
EPS = 1e-6

kernel_name = "hybrid_gla_pool_ecmoe_diffusion_step"


def rmsnorm(x, g):
    xf = x.astype(jnp.float32)
    y = xf * lax.rsqrt(jnp.mean(xf * xf, axis=-1, keepdims=True) + EPS)
    return (y * g.astype(jnp.float32)).astype(x.dtype)


def grid_pos_embed(L, dtype):
    rows = L // GRID_W
    row = jnp.repeat(jnp.arange(rows), GRID_W).astype(jnp.float32)
    col = jnp.tile(jnp.arange(GRID_W), rows).astype(jnp.float32)
    quarter = D_MODEL // 4
    omega = 1.0 / (10000.0 ** (jnp.arange(quarter, dtype=jnp.float32) / quarter))
    er = row[:, None] * omega[None, :]
    ec = col[:, None] * omega[None, :]
    return jnp.concatenate([jnp.sin(er), jnp.cos(er), jnp.sin(ec), jnp.cos(ec)], axis=-1).astype(dtype)


def gla_chunked(q, k, v, log_a, s0):
    B, H, L, DK = q.shape
    DV = v.shape[-1]
    N = L // CHUNK
    q = q.reshape(B, H, N, CHUNK, DK)
    k = k.reshape(B, H, N, CHUNK, DK)
    v = v.reshape(B, H, N, CHUNK, DV)
    b = jnp.cumsum(log_a.reshape(B, H, N, CHUNK, DK), axis=3)
    b_last = b[:, :, :, -1:, :]
    q_dec = q * jnp.exp(b)
    k_intra = k * jnp.exp(-b)
    k_state = k * jnp.exp(b_last - b)
    mask = jnp.tril(jnp.ones((CHUNK, CHUNK), dtype=bool))
    attn = jnp.where(mask, jnp.einsum("bhncd,bhnsd->bhncs", q_dec, k_intra), 0.0)
    o_intra = jnp.einsum("bhncs,bhnsv->bhncv", attn, v)
    chunk_kv = jnp.einsum("bhncd,bhncv->bhndv", k_state, v)
    decay = jnp.exp(b_last[:, :, :, 0, :])

    def step(s, inp):
        dec, kv = inp
        return dec[..., None] * s + kv, s

    s_final, s_starts = lax.scan(step, s0, (jnp.moveaxis(decay, 2, 0), jnp.moveaxis(chunk_kv, 2, 0)))
    o_inter = jnp.einsum("bhncd,nbhdv->bhncv", q_dec, s_starts)
    return (o_intra + o_inter).reshape(B, H, L, DV), s_final


def multiscale_pool(u, seg_len):
    B, L, _ = u.shape
    uf = u.astype(jnp.float32)
    cs = jnp.concatenate([jnp.zeros((B, 1, POOL_WIDTH), jnp.float32), jnp.cumsum(uf, axis=1)], axis=1)
    t = jnp.arange(L)
    seg_start = (t // seg_len) * seg_len
    seg_end = seg_start + seg_len - 1
    outs = []
    for gi, w in enumerate(POOL_WINDOWS):
        sl = slice(gi * POOL_GC, (gi + 1) * POOL_GC)
        lo = jnp.maximum(t - w // 2, seg_start)
        hi = jnp.minimum(t - w // 2 + w - 1, seg_end)
        csg = cs[:, :, sl]
        s = csg[:, hi + 1] - csg[:, lo]
        cnt = (hi - lo + 1).astype(jnp.float32)
        outs.append(s / cnt[None, :, None] - uf[:, :, sl])
    return jnp.stack(outs, axis=2)


def token_mixer(h, lp, s0_f, s0_b, seg_len):
    B, L, _ = h.shape
    f32 = jnp.float32
    proj = h @ lp["w_in"]
    offs, acc = [], 0
    for n in SPLITS[:-1]:
        acc += n
        offs.append(acc)
    q, k, v, g, za_f, za_b, u = jnp.split(proj, offs, axis=-1)

    def heads(t, d):
        return t.astype(f32).reshape(B, L, GLA_HEADS, d).transpose(0, 2, 1, 3)

    qh = heads(q, GLA_DK) * (GLA_DK ** -0.5)
    kh = heads(k, GLA_DK)
    vh = heads(v, GLA_DV)
    la_f = heads(jax.nn.log_sigmoid((za_f @ lp["w_a_up_f"] + lp["b_a_f"]).astype(f32)), GLA_DK) / GATE_NORM
    la_b = heads(jax.nn.log_sigmoid((za_b @ lp["w_a_up_b"] + lp["b_a_b"]).astype(f32)), GLA_DK) / GATE_NORM
    rev = lambda t: jnp.flip(t, axis=2)
    o_f, s_f = gla_chunked(qh, kh, vh, la_f, s0_f)
    o_b, s_b = gla_chunked(rev(qh), rev(kh), rev(vh), rev(la_b), s0_b)
    o = o_f + rev(o_b)
    o = o * lax.rsqrt(jnp.mean(o * o, axis=-1, keepdims=True) + EPS) * lp["gla_norm_g"].astype(f32)
    o = o.transpose(0, 2, 1, 3).reshape(B, L, GLA_WIDTH) * jax.nn.silu(g.astype(f32))

    p = multiscale_pool(u, seg_len)
    pool_out = jnp.einsum("blgc,gcd->blgd", p, lp["w_pool"].astype(f32)).reshape(B, L, POOL_WIDTH)
    pool_out = pool_out * lp["pool_scale"].astype(f32)

    y = jnp.concatenate([o, pool_out], axis=-1).astype(h.dtype) @ lp["w_out"]
    return y, s_f, s_b


def ec_moe(h, lp):
    B, L, D = h.shape
    n = B * L
    xf = h.reshape(n, D)
    probs = jax.nn.softmax((xf @ lp["w_router"]).astype(jnp.float32), axis=-1)
    cap = CAPACITY_FACTOR * n // N_EXPERTS
    gate_vals, tok_idx = lax.top_k(probs.T, cap)
    xe = jnp.take(xf, tok_idx, axis=0)
    hid = jax.nn.silu(jnp.einsum("ecd,edf->ecf", xe, lp["w_e_gate"])) * jnp.einsum("ecd,edf->ecf", xe, lp["w_e_up"])
    ye = jnp.einsum("ecf,efd->ecd", hid, lp["w_e_down"]) * gate_vals[..., None].astype(h.dtype)
    y = jnp.zeros_like(xf).at[tok_idx.reshape(-1)].add(ye.reshape(-1, D))
    return y.reshape(B, L, D)


def trunk_layer(x, cond, lp, s0_f, s0_b, seg_len):
    mod = jax.nn.silu(cond) @ lp["w_ada"] + lp["b_ada"]
    shift_a, scale_a, gate_a, shift_f, scale_f, gate_f = jnp.split(mod.astype(x.dtype), 6, axis=-1)
    h = rmsnorm(x, lp["norm_mix_g"]) * (1 + scale_a) + shift_a
    y, s_f, s_b = token_mixer(h, lp, s0_f, s0_b, seg_len)
    x = x + gate_a * y
    h = rmsnorm(x, lp["norm_ffn_g"]) * (1 + scale_f) + shift_f
    x = x + gate_f * ec_moe(h, lp)
    return x, s_f, s_b


def setup_inputs(seed: int = 0) -> dict:
    key = jax.random.key(seed)
    ks = jax.random.split(key, 24)
    f32 = jnp.float32
    D = D_MODEL
    nrm = lambda k, shape, s: jax.random.normal(k, shape, f32) * s
    st_shape = (DEC_BATCH, DEPTH, GLA_HEADS, GLA_DK, GLA_DV)
    return {
        "x_prompt": nrm(ks[0], (BATCH, SEQ, D), 1.0),
        "x_sample": nrm(ks[1], (DEC_BATCH, DEC_SEQ, D), 1.0),
        "state_gla_fwd": nrm(ks[2], st_shape, 1.0),
        "state_gla_bwd": nrm(ks[3], st_shape, 1.0),
        "c": nrm(ks[4], (DEC_BATCH, D), 1.0),
        "c_ctx": nrm(ks[5], (D,), 1.0),
        "norm_mix_g": 1.0 + nrm(ks[6], (DEPTH, D), 0.02),
        "w_ada": nrm(ks[7], (DEPTH, D, 6 * D), 0.5 * D ** -0.5),
        "b_ada": nrm(ks[8], (DEPTH, 6 * D), 0.02),
        "w_in": nrm(ks[9], (DEPTH, D, IN_COLS), D ** -0.5),
        "w_a_up_f": nrm(ks[10], (DEPTH, GATE_RANK, GLA_HEADS * GLA_DK), GATE_RANK ** -0.5),
        "b_a_f": nrm(ks[11], (DEPTH, GLA_HEADS * GLA_DK), 0.1),
        "w_a_up_b": nrm(ks[12], (DEPTH, GATE_RANK, GLA_HEADS * GLA_DK), GATE_RANK ** -0.5),
        "b_a_b": nrm(ks[13], (DEPTH, GLA_HEADS * GLA_DK), 0.1),
        "gla_norm_g": 1.0 + nrm(ks[14], (DEPTH, GLA_DV), 0.02),
        "w_pool": nrm(ks[15], (DEPTH, POOL_GROUPS, POOL_GC, POOL_GC), POOL_GC ** -0.5),
        "pool_scale": 1.0 + nrm(ks[16], (DEPTH, POOL_WIDTH), 0.1),
        "w_out": nrm(ks[17], (DEPTH, MIX_WIDTH, D), MIX_WIDTH ** -0.5),
        "norm_ffn_g": 1.0 + nrm(ks[18], (DEPTH, D), 0.02),
        "w_router": nrm(ks[19], (DEPTH, D, N_EXPERTS), D ** -0.5),
        "w_e_gate": nrm(ks[20], (DEPTH, N_EXPERTS, D, D_FF_EXPERT), D ** -0.5),
        "w_e_up": nrm(ks[21], (DEPTH, N_EXPERTS, D, D_FF_EXPERT), D ** -0.5),
        "w_e_down": nrm(ks[22], (DEPTH, N_EXPERTS, D_FF_EXPERT, D), D_FF_EXPERT ** -0.5),
        "final_norm_g": 1.0 + nrm(ks[23], (D,), 0.02),
    }


def reference(x_prompt, x_sample, state_gla_fwd, state_gla_bwd, c, c_ctx, norm_mix_g, w_ada, b_ada,
              w_in, w_a_up_f, b_a_f, w_a_up_b, b_a_b, gla_norm_g, w_pool, pool_scale, w_out,
              norm_ffn_g, w_router, w_e_gate, w_e_up, w_e_down, final_norm_g):
    def layer_params(l):
        return {"norm_mix_g": norm_mix_g[l], "w_ada": w_ada[l], "b_ada": b_ada[l], "w_in": w_in[l],
                "w_a_up_f": w_a_up_f[l], "b_a_f": b_a_f[l], "w_a_up_b": w_a_up_b[l], "b_a_b": b_a_b[l],
                "gla_norm_g": gla_norm_g[l], "w_pool": w_pool[l], "pool_scale": pool_scale[l],
                "w_out": w_out[l], "norm_ffn_g": norm_ffn_g[l], "w_router": w_router[l],
                "w_e_gate": w_e_gate[l], "w_e_up": w_e_up[l], "w_e_down": w_e_down[l]}

    xc = x_prompt
    Bp, Lc, _ = xc.shape
    cond_ctx = c_ctx[None, None, :]
    new_f, new_b = [], []
    for l in range(DEPTH):
        zero_state = jnp.zeros((Bp, GLA_HEADS, GLA_DK, GLA_DV), jnp.float32)
        xc, s_f, s_b = trunk_layer(xc, cond_ctx, layer_params(l), zero_state, zero_state, Lc)
        new_f.append(s_f)
        new_b.append(s_b)
    y_prompt = rmsnorm(xc, final_norm_g)
    new_state_gla_fwd = jnp.stack(new_f, axis=1)
    new_state_gla_bwd = jnp.stack(new_b, axis=1)

    Ls = x_sample.shape[1]
    xs = x_sample + grid_pos_embed(Ls, x_sample.dtype)[None]
    cond_lat = c[:, None, :]
    for l in range(DEPTH):
        s0_f = state_gla_fwd[:, l].astype(jnp.float32)
        s0_b = state_gla_bwd[:, l].astype(jnp.float32)
        xs, _, _ = trunk_layer(xs, cond_lat, layer_params(l), s0_f, s0_b, GRID_W)
    y_sample = rmsnorm(xs, final_norm_g)

    return (y_prompt, y_sample, new_state_gla_fwd, new_state_gla_bwd)
```

```python
import functools

import jax
import jax.numpy as jnp
from jax import lax
from jax.experimental import pallas as pl
from jax.experimental.pallas import tpu as pltpu

F32 = jnp.float32
BF16 = jnp.bfloat16
I32 = jnp.int32

GLA_HEADS = 4
GATE_NORM = 16.0
CHUNK = 64
POOL_WINDOWS = (2, 4, 8, 16)
CAPACITY_FACTOR = 2
GRID_W = 64
EPS = 1e-6

LANES = 128
SUBLANES_F32 = 8
VMEM_LIMIT_BYTES = 56 * 1024 * 1024

TOKEN_TILE = 512
GLA_TILE = 256
ROUTE_BLOCK = 512
ROW_CHUNK = 256
EXPERT_TILE = 512
FF_CHUNK = 256
SLAB_ALIGN = SUBLANES_F32


def _cparams(sem):
    return pltpu.CompilerParams(dimension_semantics=sem, vmem_limit_bytes=VMEM_LIMIT_BYTES)


def _rms(x, g):
    ms = jnp.mean(x * x, axis=-1, keepdims=True)
    return x * lax.rsqrt(ms + EPS) * g


def _silu(x):
    return x * jax.nn.sigmoid(x)


def _ada_kernel(c_ref, w_ref, b_ref, o_ref):
    a = _silu(c_ref[...])
    o_ref[...] = jnp.dot(a, w_ref[...], preferred_element_type=F32) + b_ref[...]


def _ada(cond, w_ada, b_ada):
    rows, d = cond.shape
    n = w_ada.shape[1]
    tn = 1536
    return pl.pallas_call(
        _ada_kernel,
        grid=(n // tn,),
        in_specs=[pl.BlockSpec((rows, d), lambda j: (0, 0)),
                  pl.BlockSpec((d, tn), lambda j: (0, j)),
                  pl.BlockSpec((1, tn), lambda j: (0, j))],
        out_specs=pl.BlockSpec((rows, tn), lambda j: (0, j)),
        out_shape=jax.ShapeDtypeStruct((rows, n), F32),
        compiler_params=_cparams(("parallel",)),
        name="ada",
    )(cond, w_ada, b_ada.reshape(1, n))


def _log_sigmoid(x):
    return jnp.minimum(x, 0.0) - jnp.log1p(jnp.exp(-jnp.abs(x)))


def _in_kernel(has_pos, dk, *refs):
    if has_pos:
        x_ref, pos_ref, *refs = refs
    else:
        x_ref, *refs = refs
    (mod_ref, gmix_ref, wmain_ref, wza_ref, wup_ref, bup_ref,
     q_ref, k_ref, v_ref, g_ref, u_ref, laf_ref, lab_ref) = refs
    x = x_ref[...]
    if has_pos:
        x = x + pos_ref[...]
    h = _rms(x, gmix_ref[...]) * (1.0 + mod_ref[1:2, :]) + mod_ref[0:1, :]
    hb = h.astype(BF16)
    p = jnp.dot(hb, wmain_ref[...], preferred_element_type=F32)
    nq = q_ref.shape[-1]
    nv = v_ref.shape[-1]
    q_ref[...] = p[:, 0:nq] * (dk ** -0.5)
    k_ref[...] = p[:, nq:2 * nq]
    v_ref[...] = p[:, 2 * nq:2 * nq + nv]
    g_ref[...] = p[:, 2 * nq + nv:2 * nq + 2 * nv]
    u_ref[...] = p[:, 2 * nq + 2 * nv:]
    za = jnp.dot(hb, wza_ref[...], preferred_element_type=F32)
    z = jnp.dot(za, wup_ref[...], preferred_element_type=F32) + bup_ref[...]
    la = _log_sigmoid(z) * (1.0 / GATE_NORM)
    laf_ref[...] = la[:, 0:nq]
    lab_ref[...] = la[:, nq:]


def _in_proj(x, pos, mod3, mod_row, gmix, wmain, wza, wup, bup, dk, nq, nv, nu):
    n, d = x.shape
    tm = TOKEN_TILE
    tok = lambda i: (i, 0)
    const = lambda i: (0, 0)
    in_specs = [pl.BlockSpec((tm, d), tok)]
    args = [x]
    if pos is not None:
        pos_tiles = pos.shape[0] // tm
        in_specs.append(pl.BlockSpec((tm, d), lambda i: (i % pos_tiles, 0)))
        args.append(pos)
    in_specs += [pl.BlockSpec((None, 6, d), lambda i: (mod_row(i), 0, 0)),
                 pl.BlockSpec((1, d), const),
                 pl.BlockSpec(wmain.shape, const),
                 pl.BlockSpec(wza.shape, const),
                 pl.BlockSpec(wup.shape, const),
                 pl.BlockSpec(bup.shape, const)]
    args += [mod3, gmix, wmain, wza, wup, bup]
    widths = (nq, nq, nv, nv, nu, nq, nq)
    return pl.pallas_call(
        functools.partial(_in_kernel, pos is not None, dk),
        grid=(n // tm,),
        in_specs=in_specs,
        out_specs=[pl.BlockSpec((tm, w), tok) for w in widths],
        out_shape=[jax.ShapeDtypeStruct((n, w), F32) for w in widths],
        compiler_params=_cparams(("parallel",)),
        name="in_proj",
    )(*args)


def _gla_direction(forward, q_ref, k_ref, v_ref, la_ref, st_ref, o_ref):
    t = q_ref.shape[0]
    pair_k = 2 * (q_ref.shape[1] // GLA_HEADS)
    pair_v = 2 * (v_ref.shape[1] // GLA_HEADS)
    nchunk = t // CHUNK
    pos = lax.broadcasted_iota(I32, (t, 1), 0) % CHUNK
    b = la_ref[...]
    s = 1
    while s < CHUNK:
        if forward:
            b = b + jnp.where(pos >= s, pltpu.roll(b, s, axis=0), 0.0)
        else:
            b = b + jnp.where(pos < CHUNK - s, pltpu.roll(b, t - s, axis=0), 0.0)
        s *= 2
    lane_k = lax.broadcasted_iota(I32, (CHUNK, pair_k), 1)
    lane_v = lax.broadcasted_iota(I32, (CHUNK, pair_v), 1)
    first_k = lane_k < pair_k // 2
    first_v = lane_v < pair_v // 2
    qi = lax.broadcasted_iota(I32, (CHUNK, 2 * CHUNK), 0)
    kj = lax.broadcasted_iota(I32, (CHUNK, 2 * CHUNK), 1) % CHUNK
    causal = (kj <= qi) if forward else (kj >= qi)
    st_row = lax.broadcasted_iota(I32, (pair_v, pair_k), 0)
    st_col = lax.broadcasted_iota(I32, (pair_v, pair_k), 1)
    diag = (st_row < pair_v // 2) == (st_col < pair_k // 2)
    order = range(nchunk) if forward else range(nchunk - 1, -1, -1)
    for c in order:
        rows = slice(c * CHUNK, (c + 1) * CHUNK)
        bc = b[rows]
        edge = bc[CHUNK - 1:CHUNK] if forward else bc[0:1]
        qd = q_ref[rows, :] * jnp.exp(bc)
        kc = k_ref[rows, :]
        ki = kc * jnp.exp(-bc)
        ks = kc * jnp.exp(edge - bc)
        dec = jnp.exp(edge)
        vc = v_ref[rows, :]
        for p in range(GLA_HEADS // 2):
            lk = slice(p * pair_k, (p + 1) * pair_k)
            lv = slice(p * pair_v, (p + 1) * pair_v)
            qd_p = qd[:, lk].astype(BF16)
            ki_p = ki[:, lk]
            ks_p = ks[:, lk].astype(BF16)
            v_p = vc[:, lv]
            kbd = jnp.concatenate([jnp.where(first_k, ki_p, 0.0), jnp.where(first_k, 0.0, ki_p)], axis=0).astype(BF16)
            attn = lax.dot_general(qd_p, kbd, (((1,), (1,)), ((), ())), preferred_element_type=F32)
            attn = jnp.where(causal, attn, 0.0).astype(BF16)
            vbd = jnp.concatenate([jnp.where(first_v, v_p, 0.0), jnp.where(first_v, 0.0, v_p)], axis=0).astype(BF16)
            st = st_ref[p]
            o = jnp.dot(attn, vbd, preferred_element_type=F32)
            o = o + lax.dot_general(qd_p, st.astype(BF16), (((1,), (1,)), ((), ())), preferred_element_type=F32)
            o_ref[rows, lv] = o
            kv = lax.dot_general(v_p.astype(BF16), ks_p, (((0,), (0,)), ((), ())), preferred_element_type=F32)
            st_ref[p] = st * dec[:, lk] + jnp.where(diag, kv, 0.0)


def _gla_kernel(qf, kf, vf, laf, qb, kb, vb, lab, s0f, s0b, of, ob, sf, sb, stf, stb):
    i = pl.program_id(1)

    @pl.when(i == 0)
    def _():
        stf[...] = s0f[...]
        stb[...] = s0b[...]

    _gla_direction(True, qf, kf, vf, laf, stf, of)
    _gla_direction(False, qb, kb, vb, lab, stb, ob)

    @pl.when(i == pl.num_programs(1) - 1)
    def _():
        sf[...] = stf[...]
        sb[...] = stb[...]


def _gla(q, k, v, laf, lab, s0f, s0b):
    bsz, l, nq = q.shape
    nv = v.shape[-1]
    t = GLA_TILE
    ns = l // t
    fwd = lambda b, i: (b, i, 0)
    bwd = lambda b, i: (b, ns - 1 - i, 0)
    st = lambda b, i: (b, 0, 0, 0)
    st_block = (None,) + s0f.shape[1:]

    def seq(width, imap):
        return pl.BlockSpec((None, t, width), imap)

    return pl.pallas_call(
        _gla_kernel,
        grid=(bsz, ns),
        in_specs=[seq(nq, fwd), seq(nq, fwd), seq(nv, fwd), seq(nq, fwd),
                  seq(nq, bwd), seq(nq, bwd), seq(nv, bwd), seq(nq, bwd),
                  pl.BlockSpec(st_block, st), pl.BlockSpec(st_block, st)],
        out_specs=[seq(nv, fwd), seq(nv, bwd), pl.BlockSpec(st_block, st), pl.BlockSpec(st_block, st)],
        out_shape=[jax.ShapeDtypeStruct((bsz, l, nv), F32), jax.ShapeDtypeStruct((bsz, l, nv), F32),
                   jax.ShapeDtypeStruct(s0f.shape, F32), jax.ShapeDtypeStruct(s0b.shape, F32)],
        scratch_shapes=[pltpu.VMEM(s0f.shape[1:], F32), pltpu.VMEM(s0b.shape[1:], F32)],
        compiler_params=_cparams(("parallel", "arbitrary")),
        name="gla",
    )(q, k, v, laf, q, k, v, lab, s0f, s0b)


def _pack_state(s):
    bsz, h, dk, dv = s.shape
    st = jnp.swapaxes(s, -1, -2).reshape(bsz, h // 2, 2, dv, dk)
    z = jnp.zeros_like(st[:, :, 0])
    top = jnp.concatenate([st[:, :, 0], z], axis=-1)
    bot = jnp.concatenate([z, st[:, :, 1]], axis=-1)
    return jnp.concatenate([top, bot], axis=-2)


def _unpack_state(sp, dk, dv):
    a = sp[:, :, :dv, :dk]
    b = sp[:, :, dv:, dk:]
    st = jnp.stack([a, b], axis=2)
    bsz, hp = sp.shape[:2]
    return jnp.swapaxes(st.reshape(bsz, 2 * hp, dv, dk), -1, -2)


def _mix_kernel(has_pos, seg_len, *refs):
    if has_pos:
        x_ref, pos_ref, *refs = refs
    else:
        x_ref, *refs = refs
    (of_ref, ob_ref, g_ref, u_ref, mod_ref, gn_ref, wpool_ref, pscale_ref, wout_ref, gffn_ref, wrt_ref,
     x1_ref, h2_ref, probs_ref) = refs
    tm = x_ref.shape[0]
    x = x_ref[...]
    if has_pos:
        x = x + pos_ref[...]
    dv = gn_ref.shape[-1]
    nv = of_ref.shape[-1]
    o = of_ref[...] + ob_ref[...]
    gn = gn_ref[...]
    y = jnp.zeros((tm, x.shape[1]), F32)
    for h in range(nv // dv):
        cols = slice(h * dv, (h + 1) * dv)
        oh = _rms(o[:, cols], gn) * _silu(g_ref[:, cols])
        y = y + jnp.dot(oh.astype(BF16), wout_ref[cols, :], preferred_element_type=F32)
    pos = lax.broadcasted_iota(I32, (tm, 1), 0) % seg_len
    gc = u_ref.shape[-1] // len(POOL_WINDOWS)
    for gi, w in enumerate(POOL_WINDOWS):
        cols = slice(gi * gc, (gi + 1) * gc)
        u = u_ref[:, cols]
        half = w // 2
        left = jnp.where(pos >= 1, pltpu.roll(u, 1, axis=0), 0.0)
        m = 1
        while m < half:
            left = left + jnp.where(pos >= m, pltpu.roll(left, m, axis=0), 0.0)
            m *= 2
        right = u
        m = 1
        while m < half:
            right = right + jnp.where(pos + m < seg_len, pltpu.roll(right, tm - m, axis=0), 0.0)
            m *= 2
        lo = jnp.maximum(pos - half, 0)
        hi = jnp.minimum(pos - half + w - 1, seg_len - 1)
        cnt = (hi - lo + 1).astype(F32)
        pooled = (left + right) / cnt - u
        po = jnp.dot(pooled.astype(BF16), wpool_ref[gi], preferred_element_type=F32) * pscale_ref[:, cols]
        y = y + jnp.dot(po.astype(BF16), wout_ref[nv + gi * gc:nv + (gi + 1) * gc, :], preferred_element_type=F32)
    x1 = x + mod_ref[2:3, :] * y
    x1_ref[...] = x1
    h2 = _rms(x1, gffn_ref[...]) * (1.0 + mod_ref[4:5, :]) + mod_ref[3:4, :]
    h2_ref[...] = h2.astype(BF16)
    logits = lax.dot_general(wrt_ref[...], h2, (((1,), (1,)), ((), ())), preferred_element_type=F32)
    mx = jnp.max(logits, axis=0, keepdims=True)
    ex = jnp.exp(logits - mx)
    probs_ref[...] = ex / jnp.sum(ex, axis=0, keepdims=True)


def _mix(x, pos, of, ob, g, u, mod3, mod_row, gn, wpool, pscale, wout, gffn, wrt, seg_len):
    n, d = x.shape
    tm = TOKEN_TILE
    ne = wrt.shape[0]
    tok = lambda i: (i, 0)
    const2 = lambda i: (0, 0)
    in_specs = [pl.BlockSpec((tm, d), tok)]
    args = [x]
    if pos is not None:
        pos_tiles = pos.shape[0] // tm
        in_specs.append(pl.BlockSpec((tm, d), lambda i: (i % pos_tiles, 0)))
        args.append(pos)
    in_specs += [pl.BlockSpec((tm, of.shape[1]), tok), pl.BlockSpec((tm, ob.shape[1]), tok),
                 pl.BlockSpec((tm, g.shape[1]), tok), pl.BlockSpec((tm, u.shape[1]), tok),
                 pl.BlockSpec((None, 6, d), lambda i: (mod_row(i), 0, 0)),
                 pl.BlockSpec(gn.shape, const2),
                 pl.BlockSpec(wpool.shape, lambda i: (0, 0, 0)),
                 pl.BlockSpec(pscale.shape, const2),
                 pl.BlockSpec(wout.shape, const2),
                 pl.BlockSpec(gffn.shape, const2),
                 pl.BlockSpec(wrt.shape, const2)]
    args += [of, ob, g, u, mod3, gn, wpool, pscale, wout, gffn, wrt]
    return pl.pallas_call(
        functools.partial(_mix_kernel, pos is not None, seg_len),
        grid=(n // tm,),
        in_specs=in_specs,
        out_specs=[pl.BlockSpec((tm, d), tok), pl.BlockSpec((tm, d), tok), pl.BlockSpec((ne, tm), lambda i: (0, i))],
        out_shape=[jax.ShapeDtypeStruct((n, d), F32), jax.ShapeDtypeStruct((n, d), BF16),
                   jax.ShapeDtypeStruct((ne, n), F32)],
        compiler_params=_cparams(("parallel",)),
        name="mix",
    )(*args)


def _topk_kernel(cap, probs_ref, gate_ref, possel_ref, posall_ref, thr_ref):
    ne, nr, _ = probs_ref.shape
    bits = pltpu.bitcast(probs_ref[...], I32)

    def bit_step(i, thr):
        cand = thr | (jnp.int32(1) << (30 - i))
        cnt = jnp.sum(jnp.where(bits >= cand, 1.0, 0.0), axis=(1, 2), keepdims=True)
        return jnp.where(cnt >= cap, cand, thr)

    thr = lax.fori_loop(0, 31, bit_step, jnp.zeros((ne, 1, 1), I32))
    thr_ref[...] = jnp.broadcast_to(thr, thr_ref.shape)

    li = lax.broadcasted_iota(I32, (LANES, LANES), 0)
    lj = lax.broadcasted_iota(I32, (LANES, LANES), 1)
    before_lane = jnp.where(li < lj, 1.0, 0.0).astype(BF16)
    ri = lax.broadcasted_iota(I32, (nr, nr), 0)
    rj = lax.broadcasted_iota(I32, (nr, nr), 1)
    before_row = jnp.where(rj < ri, 1.0, 0.0).astype(BF16)

    def excl_prefix(m):
        mb = m.astype(BF16)
        in_row = jnp.dot(mb, before_lane, preferred_element_type=F32)
        rows_before = jnp.sum(jnp.dot(before_row, mb, preferred_element_type=F32), axis=1, keepdims=True)
        return in_row + rows_before

    def per_expert(e, carry):
        p = probs_ref[e]
        be = pltpu.bitcast(p, I32)
        te = thr_ref[e][0:1, :]
        gt = be > te
        eq = be == te
        n_gt = jnp.sum(jnp.where(gt, 1.0, 0.0), axis=(0, 1), keepdims=True)
        need = cap - n_gt
        eq_rank = excl_prefix(jnp.where(eq, 1.0, 0.0))
        sel = gt | (eq & (eq_rank < need))
        pos = excl_prefix(jnp.where(sel, 1.0, 0.0)).astype(I32)
        gate_ref[e] = jnp.where(sel, p, 0.0)
        possel_ref[e] = jnp.where(sel, pos, -1)
        posall_ref[e] = pos
        return carry

    lax.fori_loop(0, ne, per_expert, 0)


def _topk(probs3, cap):
    ne, nr, _ = probs3.shape
    full = lambda: (0, 0, 0)
    spec = pl.BlockSpec(probs3.shape, full)
    return pl.pallas_call(
        functools.partial(_topk_kernel, cap),
        in_specs=[spec],
        out_specs=[spec, spec, spec],
        out_shape=[jax.ShapeDtypeStruct(probs3.shape, F32), jax.ShapeDtypeStruct(probs3.shape, I32),
                   jax.ShapeDtypeStruct(probs3.shape, I32)],
        scratch_shapes=[pltpu.VMEM((ne, SUBLANES_F32, LANES), I32)],
        compiler_params=pltpu.CompilerParams(vmem_limit_bytes=VMEM_LIMIT_BYTES),
        name="topk",
    )(probs3)


def _slab_segment(meta, e, b, nb, k):
    base_ref, ro_ref, len_ref, off_ref, _ = meta
    r0 = ro_ref[e * nb + b]
    n = len_ref[e * nb + b]
    g0 = jnp.maximum(r0, k * ROW_CHUNK)
    g1 = jnp.minimum(r0 + n, (k + 1) * ROW_CHUNK)
    cnt = pl.multiple_of(jnp.maximum(g1 - g0, 0), SLAB_ALIGN)
    src = pl.multiple_of(g0 - k * ROW_CHUNK, SLAB_ALIGN)
    dst = pl.multiple_of(off_ref[e * nb + b] + g0 - r0, SLAB_ALIGN)
    return cnt, src, dst


def _for_segments(meta, ne, b, nb, k, make_copy, action):
    def body(e, carry):
        cnt, chunk_row, region_row = _slab_segment(meta, e, b, nb, k)

        @pl.when(cnt > 0)
        def _():
            action(make_copy(e, cnt, chunk_row, region_row))
        return carry

    lax.fori_loop(0, ne, body, 0)


def _gather_kernel(ne, base_ref, ro_ref, len_ref, off_ref, tot_ref, h_ref, possel_ref, xe_hbm, buf, onehot, sem):
    meta = (base_ref, ro_ref, len_ref, off_ref, tot_ref)
    b = pl.program_id(0)
    nb = pl.num_programs(0)
    s = h_ref.shape[0]
    nch = (tot_ref[b] + ROW_CHUNK - 1) // ROW_CHUNK
    row_in_chunk = lax.broadcasted_iota(I32, (ROW_CHUNK, s), 0)

    def copies(k, slot, action):
        def make_copy(e, cnt, chunk_row, region_row):
            return pltpu.make_async_copy(buf.at[slot, pl.ds(chunk_row, cnt)],
                                         xe_hbm.at[e, pl.ds(region_row, cnt)], sem.at[slot])
        _for_segments(meta, ne, b, nb, k, make_copy, action)

    def chunk(k, carry):
        slot = k % 2

        @pl.when(k >= 2)
        def _():
            copies(k - 2, slot, lambda cp: cp.wait())

        onehot[...] = jnp.zeros(onehot.shape, F32)
        for e in range(ne):
            r0 = ro_ref[e * nb + b]
            n = len_ref[e * nb + b]

            @pl.when((r0 < (k + 1) * ROW_CHUNK) & (r0 + n > k * ROW_CHUNK))
            def _():
                ps = possel_ref[e:e + 1, :]
                row = ps - (base_ref[e * nb + b] - r0 + k * ROW_CHUNK)
                hit = (ps >= 0) & (row == row_in_chunk)
                onehot[...] = jnp.where(hit, 1.0, onehot[...])

        buf[slot] = jnp.dot(onehot[...].astype(BF16), h_ref[...], preferred_element_type=F32)
        copies(k, slot, lambda cp: cp.start())
        return carry

    lax.fori_loop(0, nch, chunk, 0)
    for back in (2, 1):
        k = nch - back

        @pl.when(k >= 0)
        def _():
            copies(k, k % 2, lambda cp: cp.wait())

    @pl.when(b == nb - 1)
    def _():
        capp = xe_hbm.shape[1]
        buf[0] = jnp.zeros(buf.shape[1:], F32)

        def tail(action):
            def per_expert(e, carry):
                used = off_ref[e * nb + nb - 1] + len_ref[e * nb + nb - 1]

                def piece(j, c):
                    start = pl.multiple_of(used + j * ROW_CHUNK, SLAB_ALIGN)
                    cnt = pl.multiple_of(jnp.minimum(ROW_CHUNK, capp - start), SLAB_ALIGN)
                    action(pltpu.make_async_copy(buf.at[0, pl.ds(0, cnt)], xe_hbm.at[e, pl.ds(start, cnt)],
                                                 sem.at[0]))
                    return c

                lax.fori_loop(0, (capp - used + ROW_CHUNK - 1) // ROW_CHUNK, piece, 0)
                return carry

            lax.fori_loop(0, ne, per_expert, 0)

        tail(lambda cp: cp.start())
        tail(lambda cp: cp.wait())


def _gather(meta, h2, possel2, ne, capp):
    n, d = h2.shape
    s = ROUTE_BLOCK
    nb = n // s
    grid_spec = pltpu.PrefetchScalarGridSpec(
        num_scalar_prefetch=5,
        grid=(nb,),
        in_specs=[pl.BlockSpec((s, d), lambda b, *_: (b, 0)),
                  pl.BlockSpec((ne, s), lambda b, *_: (0, b))],
        out_specs=pl.BlockSpec(memory_space=pl.ANY),
        scratch_shapes=[pltpu.VMEM((2, ROW_CHUNK, d), F32), pltpu.VMEM((ROW_CHUNK, s), F32),
                        pltpu.SemaphoreType.DMA((2,))],
    )
    return pl.pallas_call(
        functools.partial(_gather_kernel, ne),
        grid_spec=grid_spec,
        out_shape=jax.ShapeDtypeStruct((ne, capp, d), F32),
        compiler_params=_cparams(("arbitrary",)),
        name="gather",
    )(*meta, h2, possel2)


def _expert_kernel(used_ref, xe_ref, wg_ref, wu_ref, wd_ref, ye_ref):
    e = pl.program_id(0)
    m = pl.program_id(1)
    tm, d = xe_ref.shape
    f = wg_ref.shape[-1]
    valid = used_ref[e] - m * tm

    @pl.when(valid > 0)
    def _():
        x = xe_ref[...].astype(BF16)
        acc = jnp.zeros((tm, d), F32)
        for j in range(f // FF_CHUNK):
            cols = slice(j * FF_CHUNK, (j + 1) * FF_CHUNK)
            hg = jnp.dot(x, wg_ref[:, cols], preferred_element_type=F32)
            hu = jnp.dot(x, wu_ref[:, cols], preferred_element_type=F32)
            hid = (_silu(hg) * hu).astype(BF16)
            acc = acc + jnp.dot(hid, wd_ref[cols, :], preferred_element_type=F32)
        ye_ref[...] = acc

    @pl.when(valid <= 0)
    def _():
        ye_ref[...] = jnp.zeros(ye_ref.shape, F32)


def _experts(used, xe, wg, wu, wd):
    ne, capp, d = xe.shape
    f = wg.shape[-1]
    tm = EXPERT_TILE
    grid_spec = pltpu.PrefetchScalarGridSpec(
        num_scalar_prefetch=1,
        grid=(ne, capp // tm),
        in_specs=[pl.BlockSpec((None, tm, d), lambda e, m, *_: (e, m, 0)),
                  pl.BlockSpec((None, d, f), lambda e, m, *_: (e, 0, 0)),
                  pl.BlockSpec((None, d, f), lambda e, m, *_: (e, 0, 0)),
                  pl.BlockSpec((None, f, d), lambda e, m, *_: (e, 0, 0))],
        out_specs=pl.BlockSpec((None, tm, d), lambda e, m, *_: (e, m, 0)),
    )
    return pl.pallas_call(
        _expert_kernel,
        grid_spec=grid_spec,
        out_shape=jax.ShapeDtypeStruct((ne, capp, d), F32),
        compiler_params=_cparams(("parallel", "arbitrary")),
        name="experts",
    )(used, xe, wg, wu, wd)


def _combine_kernel(ne, final_norm, base_ref, ro_ref, len_ref, off_ref, tot_ref,
                    ye_hbm, possel_ref, gate_ref, x1_ref, mod_ref, fng_ref, o_ref, buf, weights, acc, sem):
    meta = (base_ref, ro_ref, len_ref, off_ref, tot_ref)
    b = pl.program_id(0)
    nb = pl.num_programs(0)
    s = x1_ref.shape[0]
    nch = (tot_ref[b] + ROW_CHUNK - 1) // ROW_CHUNK
    col_in_chunk = lax.broadcasted_iota(I32, (s, ROW_CHUNK), 1)

    @pl.when(b == 0)
    def _():
        buf[...] = jnp.zeros(buf.shape, F32)

    acc[...] = jnp.zeros(acc.shape, F32)

    def copies(k, slot, action):
        def make_copy(e, cnt, chunk_row, region_row):
            return pltpu.make_async_copy(ye_hbm.at[e, pl.ds(region_row, cnt)],
                                         buf.at[slot, pl.ds(chunk_row, cnt)], sem.at[slot])
        _for_segments(meta, ne, b, nb, k, make_copy, action)

    @pl.when(nch > 0)
    def _():
        copies(0, 0, lambda cp: cp.start())

    def chunk(k, carry):
        slot = k % 2

        @pl.when(k + 1 < nch)
        def _():
            copies(k + 1, 1 - slot, lambda cp: cp.start())

        weights[...] = jnp.zeros(weights.shape, F32)
        for e in range(ne):
            r0 = ro_ref[e * nb + b]
            n = len_ref[e * nb + b]

            @pl.when((r0 < (k + 1) * ROW_CHUNK) & (r0 + n > k * ROW_CHUNK))
            def _():
                ps = possel_ref[:, e:e + 1]
                col = ps - (base_ref[e * nb + b] - r0 + k * ROW_CHUNK)
                hit = (ps >= 0) & (col == col_in_chunk)
                weights[...] = jnp.where(hit, gate_ref[:, e:e + 1], weights[...])

        copies(k, slot, lambda cp: cp.wait())
        acc[...] += jnp.dot(weights[...].astype(BF16), buf[slot].astype(BF16), preferred_element_type=F32)
        return carry

    lax.fori_loop(0, nch, chunk, 0)
    x2 = x1_ref[...] + mod_ref[5:6, :] * acc[...]
    o_ref[...] = _rms(x2, fng_ref[...]) if final_norm else x2


def _combine(meta, ye, possel_t, gate_t, x1, mod3, mod_row, fng, final_norm):
    n, d = x1.shape
    ne = ye.shape[0]
    s = ROUTE_BLOCK
    nb = n // s
    per_block = s // TOKEN_TILE
    grid_spec = pltpu.PrefetchScalarGridSpec(
        num_scalar_prefetch=5,
        grid=(nb,),
        in_specs=[pl.BlockSpec(memory_space=pl.ANY),
                  pl.BlockSpec((s, ne), lambda b, *_: (b, 0)),
                  pl.BlockSpec((s, ne), lambda b, *_: (b, 0)),
                  pl.BlockSpec((s, d), lambda b, *_: (b, 0)),
                  pl.BlockSpec((None, 6, d), lambda b, *_: (mod_row(b * per_block), 0, 0)),
                  pl.BlockSpec((1, d), lambda b, *_: (0, 0))],
        out_specs=pl.BlockSpec((s, d), lambda b, *_: (b, 0)),
        scratch_shapes=[pltpu.VMEM((2, ROW_CHUNK, d), F32), pltpu.VMEM((s, ROW_CHUNK), F32),
                        pltpu.VMEM((s, d), F32), pltpu.SemaphoreType.DMA((2,))],
    )
    return pl.pallas_call(
        functools.partial(_combine_kernel, ne, final_norm),
        grid_spec=grid_spec,
        out_shape=jax.ShapeDtypeStruct((n, d), F32),
        compiler_params=_cparams(("arbitrary",)),
        name="combine",
    )(*meta, ye, possel_t, gate_t, x1, mod3, fng)


def _round_up(x, m):
    return (x + m - 1) // m * m


def _moe(x1, h2, probs_t, mod3, mod_row, wg, wu, wd, fng, final_norm):
    n, d = x1.shape
    ne = probs_t.shape[0]
    cap = CAPACITY_FACTOR * n // ne
    s = ROUTE_BLOCK
    nb = n // s
    rows_per_block = s // LANES
    gate3, possel3, posall3 = _topk(probs_t.reshape(ne, n // LANES, LANES), cap)

    base = posall3[:, ::rows_per_block, 0]
    count = jnp.concatenate([base[:, 1:], jnp.full((ne, 1), cap, I32)], axis=1) - base
    slab = (count + SLAB_ALIGN - 1) // SLAB_ALIGN * SLAB_ALIGN
    off = jnp.cumsum(slab, axis=1) - slab
    ro = jnp.cumsum(slab, axis=0) - slab
    tot = jnp.sum(slab, axis=0)
    used = jnp.sum(slab, axis=1)
    meta = (base.reshape(-1), ro.reshape(-1), slab.reshape(-1), off.reshape(-1), tot)
    capp = _round_up(cap + (SLAB_ALIGN - 1) * nb, EXPERT_TILE)

    possel2 = possel3.reshape(ne, n)
    xe = _gather(meta, h2, possel2, ne, capp)
    ye = _experts(used, xe, wg, wu, wd)
    return _combine(meta, ye, possel2.T, gate3.reshape(ne, n).T, x1, mod3, mod_row, fng, final_norm)


def _grid_pos_embed(length, d, dtype):
    rows = length // GRID_W
    row = jnp.repeat(jnp.arange(rows), GRID_W).astype(F32)
    col = jnp.tile(jnp.arange(GRID_W), rows).astype(F32)
    quarter = d // 4
    omega = 1.0 / (10000.0 ** (jnp.arange(quarter, dtype=F32) / quarter))
    er = row[:, None] * omega[None, :]
    ec = col[:, None] * omega[None, :]
    return jnp.concatenate([jnp.sin(er), jnp.cos(er), jnp.sin(ec), jnp.cos(ec)], axis=-1).astype(dtype)


def _layer(x, pos, mod3, mod_row, lp, s0f, s0b, seg_len, final_norm):
    bsz, l, d = x.shape
    n = bsz * l
    xf = x.reshape(n, d)
    nq = lp["wup"].shape[1] // 2
    nv = lp["gn"].shape[1] * GLA_HEADS
    nu = lp["pscale"].shape[1]
    dk = nq // GLA_HEADS
    q, k, v, g, u, laf, lab = _in_proj(xf, pos, mod3, mod_row, lp["gmix"], lp["wmain"], lp["wza"], lp["wup"],
                                       lp["bup"], dk, nq, nv, nu)
    seq = lambda a: a.reshape(bsz, l, a.shape[-1])
    of, ob, sf, sb = _gla(seq(q), seq(k), seq(v), seq(laf), seq(lab), s0f, s0b)
    x1, h2, probs_t = _mix(xf, pos, of.reshape(n, nv), ob.reshape(n, nv), g, u, mod3, mod_row, lp["gn"], lp["wpool"],
                           lp["pscale"], lp["wout"], lp["gffn"], lp["wrt"], seg_len)
    y = _moe(x1, h2, probs_t, mod3, mod_row, lp["wg"], lp["wu"], lp["wd"], lp["fng"], final_norm)
    return y.reshape(bsz, l, d), sf, sb


def kernel(x_prompt, x_sample, state_gla_fwd, state_gla_bwd, c, c_ctx, norm_mix_g, w_ada, b_ada, w_in, w_a_up_f, b_a_f, w_a_up_b, b_a_b, gla_norm_g, w_pool, pool_scale, w_out, norm_ffn_g, w_router, w_e_gate, w_e_up, w_e_down, final_norm_g):
    depth, d, _ = w_in.shape
    bp, lc, _ = x_prompt.shape
    bs, ls, _ = x_sample.shape
    nq = w_a_up_f.shape[-1]
    dk = nq // GLA_HEADS
    dv = gla_norm_g.shape[-1]
    nv = dv * GLA_HEADS
    rank = w_a_up_f.shape[1]
    nu = pool_scale.shape[-1]
    main_cols = 2 * nq + 2 * nv

    cond = jnp.zeros((_round_up(bs + 1, 2 * SUBLANES_F32), d), F32).at[:bs].set(c).at[bs].set(c_ctx)
    pos = _grid_pos_embed(ls, d, x_sample.dtype)
    tiles_per_seq = ls // TOKEN_TILE
    ctx_row = lambda i: bs
    lat_row = lambda i: i // tiles_per_seq

    xc, xs = x_prompt, x_sample
    new_f, new_b = [], []
    for l in range(depth):
        wi = w_in[l]
        zup = jnp.zeros((rank, nq), F32)
        lp = {
            "gmix": norm_mix_g[l].reshape(1, d),
            "wmain": jnp.concatenate([wi[:, :main_cols], wi[:, main_cols + 2 * rank:]], axis=1).astype(BF16),
            "wza": wi[:, main_cols:main_cols + 2 * rank].astype(BF16),
            "wup": jnp.concatenate([jnp.concatenate([w_a_up_f[l], zup], axis=1),
                                    jnp.concatenate([zup, w_a_up_b[l]], axis=1)], axis=0),
            "bup": jnp.concatenate([b_a_f[l], b_a_b[l]]).reshape(1, 2 * nq),
            "gn": gla_norm_g[l].reshape(1, dv),
            "wpool": w_pool[l].astype(BF16),
            "pscale": pool_scale[l].reshape(1, nu),
            "wout": w_out[l].astype(BF16),
            "gffn": norm_ffn_g[l].reshape(1, d),
            "wrt": w_router[l].T,
            "wg": w_e_gate[l].astype(BF16),
            "wu": w_e_up[l].astype(BF16),
            "wd": w_e_down[l].astype(BF16),
            "fng": final_norm_g.reshape(1, d),
        }
        mod3 = _ada(cond, w_ada[l], b_ada[l]).reshape(cond.shape[0], 6, d)
        last = l == depth - 1

        zero_state = jnp.zeros((bp, GLA_HEADS // 2, 2 * dv, 2 * dk), F32)
        xc, sf, sb = _layer(xc, None, mod3, ctx_row, lp, zero_state, zero_state, lc, last)
        new_f.append(_unpack_state(sf, dk, dv))
        new_b.append(_unpack_state(sb, dk, dv))

        s0f = _pack_state(state_gla_fwd[:, l].astype(F32))
        s0b = _pack_state(state_gla_bwd[:, l].astype(F32))
        xs, _, _ = _layer(xs, pos if l == 0 else None, mod3, lat_row, lp, s0f, s0b, GRID_W, last)

    return (xc, xs, jnp.stack(new_f, axis=1), jnp.stack(new_b, axis=1))
```

```python
import functools

import jax
import jax.numpy as jnp
from jax import lax
from jax.experimental import pallas as pl
from jax.experimental.pallas import tpu as pltpu

F32 = jnp.float32
BF16 = jnp.bfloat16
I32 = jnp.int32

GLA_HEADS = 4
GATE_NORM = 16.0
CHUNK = 64
POOL_WINDOWS = (2, 4, 8, 16)
CAPACITY_FACTOR = 2
GRID_W = 64
EPS = 1e-6

LANES = 128
SUBLANES_F32 = 8
VMEM_LIMIT_BYTES = 56 * 1024 * 1024

TOKEN_TILE = 512
GLA_TILE = 256
ROUTE_BLOCK = 512
ROW_CHUNK = 256
EXPERT_TILE = 512
FF_CHUNK = 256
SLAB_ALIGN = SUBLANES_F32


def _cparams(sem):
    return pltpu.CompilerParams(dimension_semantics=sem, vmem_limit_bytes=VMEM_LIMIT_BYTES)


def _rms(x, g):
    ms = jnp.mean(x * x, axis=-1, keepdims=True)
    return x * lax.rsqrt(ms + EPS) * g


def _silu(x):
    return x * jax.nn.sigmoid(x)


def _ada_kernel(c_ref, w_ref, b_ref, o_ref):
    a = _silu(c_ref[...])
    o_ref[...] = jnp.dot(a, w_ref[...], preferred_element_type=F32) + b_ref[...]


def _ada(cond, w_ada, b_ada):
    rows, d = cond.shape
    n = w_ada.shape[1]
    tn = 1536
    return pl.pallas_call(
        _ada_kernel,
        grid=(n // tn,),
        in_specs=[pl.BlockSpec((rows, d), lambda j: (0, 0)),
                  pl.BlockSpec((d, tn), lambda j: (0, j)),
                  pl.BlockSpec((1, tn), lambda j: (0, j))],
        out_specs=pl.BlockSpec((rows, tn), lambda j: (0, j)),
        out_shape=jax.ShapeDtypeStruct((rows, n), F32),
        compiler_params=_cparams(("parallel",)),
        name="ada",
    )(cond, w_ada, b_ada.reshape(1, n))


def _log_sigmoid(x):
    return jnp.minimum(x, 0.0) - jnp.log1p(jnp.exp(-jnp.abs(x)))


def _in_kernel(has_pos, dk, *refs):
    if has_pos:
        x_ref, pos_ref, *refs = refs
    else:
        x_ref, *refs = refs
    (mod_ref, gmix_ref, wmain_ref, wza_ref, wup_ref, bup_ref,
     q_ref, k_ref, v_ref, g_ref, u_ref, laf_ref, lab_ref) = refs
    x = x_ref[...]
    if has_pos:
        x = x + pos_ref[...]
    h = _rms(x, gmix_ref[...]) * (1.0 + mod_ref[1:2, :]) + mod_ref[0:1, :]
    hb = h.astype(BF16)
    p = jnp.dot(hb, wmain_ref[...], preferred_element_type=F32)
    nq = q_ref.shape[-1]
    nv = v_ref.shape[-1]
    q_ref[...] = p[:, 0:nq] * (dk ** -0.5)
    k_ref[...] = p[:, nq:2 * nq]
    v_ref[...] = p[:, 2 * nq:2 * nq + nv]
    g_ref[...] = p[:, 2 * nq + nv:2 * nq + 2 * nv]
    u_ref[...] = p[:, 2 * nq + 2 * nv:]
    za = jnp.dot(hb, wza_ref[...], preferred_element_type=F32)
    z = jnp.dot(za, wup_ref[...], preferred_element_type=F32) + bup_ref[...]
    la = _log_sigmoid(z) * (1.0 / GATE_NORM)
    laf_ref[...] = la[:, 0:nq]
    lab_ref[...] = la[:, nq:]


def _in_proj(x, pos, mod3, mod_row, gmix, wmain, wza, wup, bup, dk, nq, nv, nu):
    n, d = x.shape
    tm = TOKEN_TILE
    tok = lambda i: (i, 0)
    const = lambda i: (0, 0)
    in_specs = [pl.BlockSpec((tm, d), tok)]
    args = [x]
    if pos is not None:
        pos_tiles = pos.shape[0] // tm
        in_specs.append(pl.BlockSpec((tm, d), lambda i: (i % pos_tiles, 0)))
        args.append(pos)
    in_specs += [pl.BlockSpec((None, 6, d), lambda i: (mod_row(i), 0, 0)),
                 pl.BlockSpec((1, d), const),
                 pl.BlockSpec(wmain.shape, const),
                 pl.BlockSpec(wza.shape, const),
                 pl.BlockSpec(wup.shape, const),
                 pl.BlockSpec(bup.shape, const)]
    args += [mod3, gmix, wmain, wza, wup, bup]
    widths = (nq, nq, nv, nv, nu, nq, nq)
    return pl.pallas_call(
        functools.partial(_in_kernel, pos is not None, dk),
        grid=(n // tm,),
        in_specs=in_specs,
        out_specs=[pl.BlockSpec((tm, w), tok) for w in widths],
        out_shape=[jax.ShapeDtypeStruct((n, w), F32) for w in widths],
        compiler_params=_cparams(("parallel",)),
        name="in_proj",
    )(*args)


def _gla_direction(forward, q_ref, k_ref, v_ref, la_ref, st_ref, o_ref):
    t = q_ref.shape[0]
    pair_k = 2 * (q_ref.shape[1] // GLA_HEADS)
    pair_v = 2 * (v_ref.shape[1] // GLA_HEADS)
    nchunk = t // CHUNK
    pos = lax.broadcasted_iota(I32, (t, 1), 0) % CHUNK
    b = la_ref[...]
    s = 1
    while s < CHUNK:
        if forward:
            b = b + jnp.where(pos >= s, pltpu.roll(b, s, axis=0), 0.0)
        else:
            b = b + jnp.where(pos < CHUNK - s, pltpu.roll(b, t - s, axis=0), 0.0)
        s *= 2
    lane_k = lax.broadcasted_iota(I32, (CHUNK, pair_k), 1)
    lane_v = lax.broadcasted_iota(I32, (CHUNK, pair_v), 1)
    first_k = lane_k < pair_k // 2
    first_v = lane_v < pair_v // 2
    qi = lax.broadcasted_iota(I32, (CHUNK, 2 * CHUNK), 0)
    kj = lax.broadcasted_iota(I32, (CHUNK, 2 * CHUNK), 1) % CHUNK
    causal = (kj <= qi) if forward else (kj >= qi)
    st_row = lax.broadcasted_iota(I32, (pair_v, pair_k), 0)
    st_col = lax.broadcasted_iota(I32, (pair_v, pair_k), 1)
    diag = (st_row < pair_v // 2) == (st_col < pair_k // 2)
    order = range(nchunk) if forward else range(nchunk - 1, -1, -1)
    for c in order:
        rows = slice(c * CHUNK, (c + 1) * CHUNK)
        bc = b[rows]
        edge = bc[CHUNK - 1:CHUNK] if forward else bc[0:1]
        qd = q_ref[rows, :] * jnp.exp(bc)
        kc = k_ref[rows, :]
        ki = kc * jnp.exp(-bc)
        ks = kc * jnp.exp(edge - bc)
        dec = jnp.exp(edge)
        vc = v_ref[rows, :]
        for p in range(GLA_HEADS // 2):
            lk = slice(p * pair_k, (p + 1) * pair_k)
            lv = slice(p * pair_v, (p + 1) * pair_v)
            qd_p = qd[:, lk].astype(BF16)
            ki_p = ki[:, lk]
            ks_p = ks[:, lk].astype(BF16)
            v_p = vc[:, lv]
            kbd = jnp.concatenate([jnp.where(first_k, ki_p, 0.0), jnp.where(first_k, 0.0, ki_p)], axis=0).astype(BF16)
            attn = lax.dot_general(qd_p, kbd, (((1,), (1,)), ((), ())), preferred_element_type=F32)
            attn = jnp.where(causal, attn, 0.0).astype(BF16)
            vbd = jnp.concatenate([jnp.where(first_v, v_p, 0.0), jnp.where(first_v, 0.0, v_p)], axis=0).astype(BF16)
            st = st_ref[p]
            o = jnp.dot(attn, vbd, preferred_element_type=F32)
            o = o + lax.dot_general(qd_p, st.astype(BF16), (((1,), (1,)), ((), ())), preferred_element_type=F32)
            o_ref[rows, lv] = o
            kv = lax.dot_general(v_p.astype(BF16), ks_p, (((0,), (0,)), ((), ())), preferred_element_type=F32)
            st_ref[p] = st * dec[:, lk] + jnp.where(diag, kv, 0.0)


def _gla_kernel(qf, kf, vf, laf, qb, kb, vb, lab, s0f, s0b, of, ob, sf, sb, stf, stb):
    i = pl.program_id(1)

    @pl.when(i == 0)
    def _():
        stf[...] = s0f[...]
        stb[...] = s0b[...]

    _gla_direction(True, qf, kf, vf, laf, stf, of)
    _gla_direction(False, qb, kb, vb, lab, stb, ob)

    @pl.when(i == pl.num_programs(1) - 1)
    def _():
        sf[...] = stf[...]
        sb[...] = stb[...]


def _gla(q, k, v, laf, lab, s0f, s0b):
    bsz, l, nq = q.shape
    nv = v.shape[-1]
    t = GLA_TILE
    ns = l // t
    fwd = lambda b, i: (b, i, 0)
    bwd = lambda b, i: (b, ns - 1 - i, 0)
    st = lambda b, i: (b, 0, 0, 0)
    st_block = (None,) + s0f.shape[1:]

    def seq(width, imap):
        return pl.BlockSpec((None, t, width), imap)

    return pl.pallas_call(
        _gla_kernel,
        grid=(bsz, ns),
        in_specs=[seq(nq, fwd), seq(nq, fwd), seq(nv, fwd), seq(nq, fwd),
                  seq(nq, bwd), seq(nq, bwd), seq(nv, bwd), seq(nq, bwd),
                  pl.BlockSpec(st_block, st), pl.BlockSpec(st_block, st)],
        out_specs=[seq(nv, fwd), seq(nv, bwd), pl.BlockSpec(st_block, st), pl.BlockSpec(st_block, st)],
        out_shape=[jax.ShapeDtypeStruct((bsz, l, nv), F32), jax.ShapeDtypeStruct((bsz, l, nv), F32),
                   jax.ShapeDtypeStruct(s0f.shape, F32), jax.ShapeDtypeStruct(s0b.shape, F32)],
        scratch_shapes=[pltpu.VMEM(s0f.shape[1:], F32), pltpu.VMEM(s0b.shape[1:], F32)],
        compiler_params=_cparams(("parallel", "arbitrary")),
        name="gla",
    )(q, k, v, laf, q, k, v, lab, s0f, s0b)


def _pack_state(s):
    bsz, h, dk, dv = s.shape
    st = jnp.swapaxes(s, -1, -2).reshape(bsz, h // 2, 2, dv, dk)
    z = jnp.zeros_like(st[:, :, 0])
    top = jnp.concatenate([st[:, :, 0], z], axis=-1)
    bot = jnp.concatenate([z, st[:, :, 1]], axis=-1)
    return jnp.concatenate([top, bot], axis=-2)


def _unpack_state(sp, dk, dv):
    a = sp[:, :, :dv, :dk]
    b = sp[:, :, dv:, dk:]
    st = jnp.stack([a, b], axis=2)
    bsz, hp = sp.shape[:2]
    return jnp.swapaxes(st.reshape(bsz, 2 * hp, dv, dk), -1, -2)


def _mix_kernel(has_pos, seg_len, *refs):
    if has_pos:
        x_ref, pos_ref, *refs = refs
    else:
        x_ref, *refs = refs
    (of_ref, ob_ref, g_ref, u_ref, mod_ref, gn_ref, wpool_ref, pscale_ref, wout_ref, gffn_ref, wrt_ref,
     x1_ref, h2_ref, probs_ref) = refs
    tm = x_ref.shape[0]
    x = x_ref[...]
    if has_pos:
        x = x + pos_ref[...]
    dv = gn_ref.shape[-1]
    nv = of_ref.shape[-1]
    o = of_ref[...] + ob_ref[...]
    gn = gn_ref[...]
    y = jnp.zeros((tm, x.shape[1]), F32)
    for h in range(nv // dv):
        cols = slice(h * dv, (h + 1) * dv)
        oh = _rms(o[:, cols], gn) * _silu(g_ref[:, cols])
        y = y + jnp.dot(oh.astype(BF16), wout_ref[cols, :], preferred_element_type=F32)
    pos = lax.broadcasted_iota(I32, (tm, 1), 0) % seg_len
    gc = u_ref.shape[-1] // len(POOL_WINDOWS)
    for gi, w in enumerate(POOL_WINDOWS):
        cols = slice(gi * gc, (gi + 1) * gc)
        u = u_ref[:, cols]
        half = w // 2
        left = jnp.where(pos >= 1, pltpu.roll(u, 1, axis=0), 0.0)
        m = 1
        while m < half:
            left = left + jnp.where(pos >= m, pltpu.roll(left, m, axis=0), 0.0)
            m *= 2
        right = u
        m = 1
        while m < half:
            right = right + jnp.where(pos + m < seg_len, pltpu.roll(right, tm - m, axis=0), 0.0)
            m *= 2
        lo = jnp.maximum(pos - half, 0)
        hi = jnp.minimum(pos - half + w - 1, seg_len - 1)
        cnt = (hi - lo + 1).astype(F32)
        pooled = (left + right) / cnt - u
        po = jnp.dot(pooled.astype(BF16), wpool_ref[gi], preferred_element_type=F32) * pscale_ref[:, cols]
        y = y + jnp.dot(po.astype(BF16), wout_ref[nv + gi * gc:nv + (gi + 1) * gc, :], preferred_element_type=F32)
    x1 = x + mod_ref[2:3, :] * y
    x1_ref[...] = x1
    h2 = _rms(x1, gffn_ref[...]) * (1.0 + mod_ref[4:5, :]) + mod_ref[3:4, :]
    h2_ref[...] = h2.astype(BF16)
    logits = lax.dot_general(wrt_ref[...], h2, (((1,), (1,)), ((), ())), preferred_element_type=F32)
    mx = jnp.max(logits, axis=0, keepdims=True)
    ex = jnp.exp(logits - mx)
    probs_ref[...] = ex / jnp.sum(ex, axis=0, keepdims=True)


def _mix(x, pos, of, ob, g, u, mod3, mod_row, gn, wpool, pscale, wout, gffn, wrt, seg_len):
    n, d = x.shape
    tm = TOKEN_TILE
    ne = wrt.shape[0]
    tok = lambda i: (i, 0)
    const2 = lambda i: (0, 0)
    in_specs = [pl.BlockSpec((tm, d), tok)]
    args = [x]
    if pos is not None:
        pos_tiles = pos.shape[0] // tm
        in_specs.append(pl.BlockSpec((tm, d), lambda i: (i % pos_tiles, 0)))
        args.append(pos)
    in_specs += [pl.BlockSpec((tm, of.shape[1]), tok), pl.BlockSpec((tm, ob.shape[1]), tok),
                 pl.BlockSpec((tm, g.shape[1]), tok), pl.BlockSpec((tm, u.shape[1]), tok),
                 pl.BlockSpec((None, 6, d), lambda i: (mod_row(i), 0, 0)),
                 pl.BlockSpec(gn.shape, const2),
                 pl.BlockSpec(wpool.shape, lambda i: (0, 0, 0)),
                 pl.BlockSpec(pscale.shape, const2),
                 pl.BlockSpec(wout.shape, const2),
                 pl.BlockSpec(gffn.shape, const2),
                 pl.BlockSpec(wrt.shape, const2)]
    args += [of, ob, g, u, mod3, gn, wpool, pscale, wout, gffn, wrt]
    return pl.pallas_call(
        functools.partial(_mix_kernel, pos is not None, seg_len),
        grid=(n // tm,),
        in_specs=in_specs,
        out_specs=[pl.BlockSpec((tm, d), tok), pl.BlockSpec((tm, d), tok), pl.BlockSpec((ne, tm), lambda i: (0, i))],
        out_shape=[jax.ShapeDtypeStruct((n, d), F32), jax.ShapeDtypeStruct((n, d), BF16),
                   jax.ShapeDtypeStruct((ne, n), F32)],
        compiler_params=_cparams(("parallel",)),
        name="mix",
    )(*args)


def _topk_kernel(cap, probs_ref, gate_ref, possel_ref, posall_ref, thr_ref):
    ne, nr, _ = probs_ref.shape
    probs = probs_ref[...]

    def bit_step(i, thr):
        cand = thr | (jnp.int32(1) << (30 - i))
        cnt = jnp.sum(jnp.where(probs >= pltpu.bitcast(cand, F32), 1.0, 0.0), axis=(1, 2), keepdims=True)
        return jnp.where(cnt >= cap, cand, thr)

    thr = lax.fori_loop(0, 31, bit_step, jnp.zeros((ne, 1, 1), I32))
    thr_ref[...] = jnp.broadcast_to(pltpu.bitcast(thr, F32), thr_ref.shape)

    li = lax.broadcasted_iota(I32, (LANES, LANES), 0)
    lj = lax.broadcasted_iota(I32, (LANES, LANES), 1)
    before_lane = jnp.where(li < lj, 1.0, 0.0).astype(BF16)
    ri = lax.broadcasted_iota(I32, (nr, nr), 0)
    rj = lax.broadcasted_iota(I32, (nr, nr), 1)
    before_row = jnp.where(rj < ri, 1.0, 0.0).astype(BF16)

    def excl_prefix(m):
        mb = m.astype(BF16)
        in_row = jnp.dot(mb, before_lane, preferred_element_type=F32)
        rows_before = jnp.sum(jnp.dot(before_row, mb, preferred_element_type=F32), axis=1, keepdims=True)
        return in_row + rows_before

    def per_expert(e, carry):
        p = probs_ref[e]
        at_least = p >= thr_ref[e][0:1, :]
        kth = jnp.min(jnp.where(at_least, p, jnp.inf), axis=(0, 1), keepdims=True)
        gt = p > kth
        eq = p == kth
        n_gt = jnp.sum(jnp.where(gt, 1.0, 0.0), axis=(0, 1), keepdims=True)
        need = cap - n_gt
        eq_rank = excl_prefix(jnp.where(eq, 1.0, 0.0))
        sel = gt | (eq & (eq_rank < need))
        pos = excl_prefix(jnp.where(sel, 1.0, 0.0)).astype(I32)
        gate_ref[e] = jnp.where(sel, p, 0.0)
        possel_ref[e] = jnp.where(sel, pos, -1)
        posall_ref[e] = pos
        return carry

    lax.fori_loop(0, ne, per_expert, 0)


def _topk(probs3, cap):
    ne, nr, _ = probs3.shape
    full = lambda: (0, 0, 0)
    spec = pl.BlockSpec(probs3.shape, full)
    return pl.pallas_call(
        functools.partial(_topk_kernel, cap),
        in_specs=[spec],
        out_specs=[spec, spec, spec],
        out_shape=[jax.ShapeDtypeStruct(probs3.shape, F32), jax.ShapeDtypeStruct(probs3.shape, I32),
                   jax.ShapeDtypeStruct(probs3.shape, I32)],
        scratch_shapes=[pltpu.VMEM((ne, SUBLANES_F32, LANES), F32)],
        compiler_params=pltpu.CompilerParams(vmem_limit_bytes=VMEM_LIMIT_BYTES),
        name="topk",
    )(probs3)


def _num_chunks(meta, b):
    return (meta[4][b] + ROW_CHUNK - 1) // ROW_CHUNK


def _for_segments(meta, ne, b, nb, k, make_copy, action):
    _, ro_ref, len_ref, off_ref, _ = meta

    def body(e, carry):
        r0 = ro_ref[e * nb + b]
        g0 = jnp.maximum(r0, k * ROW_CHUNK)
        g1 = jnp.minimum(r0 + len_ref[e * nb + b], (k + 1) * ROW_CHUNK)
        cnt = pl.multiple_of(jnp.maximum(g1 - g0, 0), SLAB_ALIGN)
        chunk_row = pl.multiple_of(g0 - k * ROW_CHUNK, SLAB_ALIGN)
        region_row = pl.multiple_of(off_ref[e * nb + b] + g0 - r0, SLAB_ALIGN)

        @pl.when(cnt > 0)
        def _():
            action(make_copy(e, cnt, chunk_row, region_row))
        return carry

    lax.fori_loop(0, ne, body, 0)


def _build_rows(meta, ne, b, nb, k, possel_ref, value_ref, dst):
    base_ref, ro_ref, len_ref, _, tot_ref = meta
    s = dst.shape[1]
    c0 = k * ROW_CHUNK
    sub = lax.broadcasted_iota(I32, (SLAB_ALIGN, s), 0)

    def per_expert(e, carry):
        r0 = ro_ref[e * nb + b]
        lo = jnp.maximum(r0, c0)
        hi = jnp.minimum(r0 + len_ref[e * nb + b], c0 + ROW_CHUNK)
        ps = possel_ref[pl.ds(e, 1), :]
        rel = jnp.where(ps >= 0, ps - (base_ref[e * nb + b] - r0 + c0), -1)
        val = 1.0 if value_ref is None else value_ref[pl.ds(e, 1), :]

        def group(j, c):
            row = pl.multiple_of(lo - c0 + j * SLAB_ALIGN, SLAB_ALIGN)
            dst[pl.ds(row, SLAB_ALIGN), :] = jnp.where((rel - row) == sub, val, 0.0)
            return c

        lax.fori_loop(0, jnp.maximum(hi - lo, 0) // SLAB_ALIGN, group, 0)
        return carry

    lax.fori_loop(0, ne, per_expert, 0)
    end = jnp.clip(tot_ref[b] - c0, 0, ROW_CHUNK)

    def clear(j, c):
        row = pl.multiple_of(end + j * SLAB_ALIGN, SLAB_ALIGN)
        dst[pl.ds(row, SLAB_ALIGN), :] = jnp.zeros((SLAB_ALIGN, s), F32)
        return c

    lax.fori_loop(0, (ROW_CHUNK - end) // SLAB_ALIGN, clear, 0)


def _gather_kernel(ne, base_ref, ro_ref, len_ref, off_ref, tot_ref, h_ref, possel_ref, xe_hbm, buf, onehot, sem):
    meta = (base_ref, ro_ref, len_ref, off_ref, tot_ref)
    b = pl.program_id(0)
    nb = pl.num_programs(0)
    nch = _num_chunks(meta, b)

    def copies(blk, k, slot, action):
        def make_copy(e, cnt, chunk_row, region_row):
            return pltpu.make_async_copy(buf.at[slot, pl.ds(chunk_row, cnt)],
                                         xe_hbm.at[e, pl.ds(region_row, cnt)], sem.at[slot])
        _for_segments(meta, ne, blk, nb, k, make_copy, action)

    def drain(blk):
        for back in (2, 1):
            k = _num_chunks(meta, blk) - back

            @pl.when(k >= 0)
            def _():
                copies(blk, k, k % 2, lambda cp: cp.wait())

    prev = jnp.maximum(b - 1, 0)

    def chunk(k, carry):
        slot = k % 2
        _build_rows(meta, ne, b, nb, k, possel_ref, None, onehot)

        @pl.when(k >= 2)
        def _():
            copies(b, k - 2, slot, lambda cp: cp.wait())

        @pl.when((k == 0) & (b > 0))
        def _():
            drain(prev)

        buf[slot] = jnp.dot(onehot[...].astype(BF16), h_ref[...], preferred_element_type=F32)
        copies(b, k, slot, lambda cp: cp.start())
        return carry

    lax.fori_loop(0, nch, chunk, 0)

    @pl.when((nch == 0) & (b > 0))
    def _():
        drain(prev)

    @pl.when(b == nb - 1)
    def _():
        drain(b)
        capp = xe_hbm.shape[1]
        buf[0] = jnp.zeros(buf.shape[1:], F32)

        def tail(action):
            def per_expert(e, carry):
                used = off_ref[e * nb + nb - 1] + len_ref[e * nb + nb - 1]

                def piece(j, c):
                    start = pl.multiple_of(used + j * ROW_CHUNK, SLAB_ALIGN)
                    cnt = pl.multiple_of(jnp.minimum(ROW_CHUNK, capp - start), SLAB_ALIGN)
                    action(pltpu.make_async_copy(buf.at[0, pl.ds(0, cnt)], xe_hbm.at[e, pl.ds(start, cnt)],
                                                 sem.at[0]))
                    return c

                lax.fori_loop(0, (capp - used + ROW_CHUNK - 1) // ROW_CHUNK, piece, 0)
                return carry

            lax.fori_loop(0, ne, per_expert, 0)

        tail(lambda cp: cp.start())
        tail(lambda cp: cp.wait())


def _gather(meta, h2, possel2, ne, capp):
    n, d = h2.shape
    s = ROUTE_BLOCK
    nb = n // s
    grid_spec = pltpu.PrefetchScalarGridSpec(
        num_scalar_prefetch=5,
        grid=(nb,),
        in_specs=[pl.BlockSpec((s, d), lambda b, *_: (b, 0)),
                  pl.BlockSpec((ne, s), lambda b, *_: (0, b))],
        out_specs=pl.BlockSpec(memory_space=pl.ANY),
        scratch_shapes=[pltpu.VMEM((2, ROW_CHUNK, d), F32), pltpu.VMEM((ROW_CHUNK, s), F32),
                        pltpu.SemaphoreType.DMA((2,))],
    )
    return pl.pallas_call(
        functools.partial(_gather_kernel, ne),
        grid_spec=grid_spec,
        out_shape=jax.ShapeDtypeStruct((ne, capp, d), F32),
        compiler_params=_cparams(("arbitrary",)),
        name="gather",
    )(*meta, h2, possel2)


def _expert_kernel(used_ref, xe_ref, wg_ref, wu_ref, wd_ref, ye_ref):
    e = pl.program_id(0)
    m = pl.program_id(1)
    tm, d = xe_ref.shape
    f = wg_ref.shape[-1]
    valid = used_ref[e] - m * tm

    @pl.when(valid > 0)
    def _():
        x = xe_ref[...].astype(BF16)
        acc = jnp.zeros((tm, d), F32)
        for j in range(f // FF_CHUNK):
            cols = slice(j * FF_CHUNK, (j + 1) * FF_CHUNK)
            hg = jnp.dot(x, wg_ref[:, cols], preferred_element_type=F32)
            hu = jnp.dot(x, wu_ref[:, cols], preferred_element_type=F32)
            hid = (_silu(hg) * hu).astype(BF16)
            acc = acc + jnp.dot(hid, wd_ref[cols, :], preferred_element_type=F32)
        ye_ref[...] = acc

    @pl.when(valid <= 0)
    def _():
        ye_ref[...] = jnp.zeros(ye_ref.shape, F32)


def _experts(used, xe, wg, wu, wd):
    ne, capp, d = xe.shape
    f = wg.shape[-1]
    tm = EXPERT_TILE
    grid_spec = pltpu.PrefetchScalarGridSpec(
        num_scalar_prefetch=1,
        grid=(ne, capp // tm),
        in_specs=[pl.BlockSpec((None, tm, d), lambda e, m, *_: (e, m, 0)),
                  pl.BlockSpec((None, d, f), lambda e, m, *_: (e, 0, 0)),
                  pl.BlockSpec((None, d, f), lambda e, m, *_: (e, 0, 0)),
                  pl.BlockSpec((None, f, d), lambda e, m, *_: (e, 0, 0))],
        out_specs=pl.BlockSpec((None, tm, d), lambda e, m, *_: (e, m, 0)),
    )
    return pl.pallas_call(
        _expert_kernel,
        grid_spec=grid_spec,
        out_shape=jax.ShapeDtypeStruct((ne, capp, d), F32),
        compiler_params=_cparams(("parallel", "arbitrary")),
        name="experts",
    )(used, xe, wg, wu, wd)


def _combine_kernel(ne, final_norm, base_ref, ro_ref, len_ref, off_ref, tot_ref,
                    ye_hbm, possel_ref, gate_ref, x1_ref, mod_ref, fng_ref, o_ref, buf, weights, acc, sem):
    meta = (base_ref, ro_ref, len_ref, off_ref, tot_ref)
    b = pl.program_id(0)
    nb = pl.num_programs(0)
    nch = _num_chunks(meta, b)

    def copies(blk, k, slot, action):
        def make_copy(e, cnt, chunk_row, region_row):
            return pltpu.make_async_copy(ye_hbm.at[e, pl.ds(region_row, cnt)],
                                         buf.at[slot, pl.ds(chunk_row, cnt)], sem.at[slot])
        _for_segments(meta, ne, blk, nb, k, make_copy, action)

    @pl.when(b == 0)
    def _():
        buf[...] = jnp.zeros(buf.shape, F32)

        @pl.when(nch > 0)
        def _():
            copies(b, 0, 0, lambda cp: cp.start())

    acc[...] = jnp.zeros(acc.shape, F32)

    def chunk(k, carry):
        slot = k % 2

        @pl.when(k + 1 < nch)
        def _():
            copies(b, k + 1, 1 - slot, lambda cp: cp.start())

        _build_rows(meta, ne, b, nb, k, possel_ref, gate_ref, weights)
        copies(b, k, slot, lambda cp: cp.wait())
        acc[...] += lax.dot_general(weights[...].astype(BF16), buf[slot].astype(BF16), (((0,), (0,)), ((), ())),
                                    preferred_element_type=F32)
        return carry

    lax.fori_loop(0, nch, chunk, 0)

    nxt = jnp.minimum(b + 1, nb - 1)

    @pl.when((b + 1 < nb) & (_num_chunks(meta, nxt) > 0))
    def _():
        copies(nxt, 0, 0, lambda cp: cp.start())

    x2 = x1_ref[...] + mod_ref[5:6, :] * acc[...]
    o_ref[...] = _rms(x2, fng_ref[...]) if final_norm else x2


def _combine(meta, ye, possel2, gate2, x1, mod3, mod_row, fng, final_norm):
    n, d = x1.shape
    ne = ye.shape[0]
    s = ROUTE_BLOCK
    nb = n // s
    per_block = s // TOKEN_TILE
    grid_spec = pltpu.PrefetchScalarGridSpec(
        num_scalar_prefetch=5,
        grid=(nb,),
        in_specs=[pl.BlockSpec(memory_space=pl.ANY),
                  pl.BlockSpec((ne, s), lambda b, *_: (0, b)),
                  pl.BlockSpec((ne, s), lambda b, *_: (0, b)),
                  pl.BlockSpec((s, d), lambda b, *_: (b, 0)),
                  pl.BlockSpec((None, 6, d), lambda b, *_: (mod_row(b * per_block), 0, 0)),
                  pl.BlockSpec((1, d), lambda b, *_: (0, 0))],
        out_specs=pl.BlockSpec((s, d), lambda b, *_: (b, 0)),
        scratch_shapes=[pltpu.VMEM((2, ROW_CHUNK, d), F32), pltpu.VMEM((ROW_CHUNK, s), F32),
                        pltpu.VMEM((s, d), F32), pltpu.SemaphoreType.DMA((2,))],
    )
    return pl.pallas_call(
        functools.partial(_combine_kernel, ne, final_norm),
        grid_spec=grid_spec,
        out_shape=jax.ShapeDtypeStruct((n, d), F32),
        compiler_params=_cparams(("arbitrary",)),
        name="combine",
    )(*meta, ye, possel2, gate2, x1, mod3, fng)


def _round_up(x, m):
    return (x + m - 1) // m * m


def _moe(x1, h2, probs_t, mod3, mod_row, wg, wu, wd, fng, final_norm):
    n, d = x1.shape
    ne = probs_t.shape[0]
    cap = CAPACITY_FACTOR * n // ne
    s = ROUTE_BLOCK
    nb = n // s
    rows_per_block = s // LANES
    gate3, possel3, posall3 = _topk(probs_t.reshape(ne, n // LANES, LANES), cap)

    base = posall3[:, ::rows_per_block, 0]
    count = jnp.concatenate([base[:, 1:], jnp.full((ne, 1), cap, I32)], axis=1) - base
    slab = (count + SLAB_ALIGN - 1) // SLAB_ALIGN * SLAB_ALIGN
    off = jnp.cumsum(slab, axis=1) - slab
    ro = jnp.cumsum(slab, axis=0) - slab
    tot = jnp.sum(slab, axis=0)
    used = jnp.sum(slab, axis=1)
    meta = (base.reshape(-1), ro.reshape(-1), slab.reshape(-1), off.reshape(-1), tot)
    capp = _round_up(cap + (SLAB_ALIGN - 1) * nb, EXPERT_TILE)

    possel2 = possel3.reshape(ne, n)
    xe = _gather(meta, h2, possel2, ne, capp)
    ye = _experts(used, xe, wg, wu, wd)
    return _combine(meta, ye, possel2, gate3.reshape(ne, n), x1, mod3, mod_row, fng, final_norm)


def _grid_pos_embed(length, d, dtype):
    rows = length // GRID_W
    row = jnp.repeat(jnp.arange(rows), GRID_W).astype(F32)
    col = jnp.tile(jnp.arange(GRID_W), rows).astype(F32)
    quarter = d // 4
    omega = 1.0 / (10000.0 ** (jnp.arange(quarter, dtype=F32) / quarter))
    er = row[:, None] * omega[None, :]
    ec = col[:, None] * omega[None, :]
    return jnp.concatenate([jnp.sin(er), jnp.cos(er), jnp.sin(ec), jnp.cos(ec)], axis=-1).astype(dtype)


def _layer(x, pos, mod3, mod_row, lp, s0f, s0b, seg_len, final_norm):
    bsz, l, d = x.shape
    n = bsz * l
    xf = x.reshape(n, d)
    nq = lp["wup"].shape[1] // 2
    nv = lp["gn"].shape[1] * GLA_HEADS
    nu = lp["pscale"].shape[1]
    dk = nq // GLA_HEADS
    q, k, v, g, u, laf, lab = _in_proj(xf, pos, mod3, mod_row, lp["gmix"], lp["wmain"], lp["wza"], lp["wup"],
                                       lp["bup"], dk, nq, nv, nu)
    seq = lambda a: a.reshape(bsz, l, a.shape[-1])
    of, ob, sf, sb = _gla(seq(q), seq(k), seq(v), seq(laf), seq(lab), s0f, s0b)
    x1, h2, probs_t = _mix(xf, pos, of.reshape(n, nv), ob.reshape(n, nv), g, u, mod3, mod_row, lp["gn"], lp["wpool"],
                           lp["pscale"], lp["wout"], lp["gffn"], lp["wrt"], seg_len)
    y = _moe(x1, h2, probs_t, mod3, mod_row, lp["wg"], lp["wu"], lp["wd"], lp["fng"], final_norm)
    return y.reshape(bsz, l, d), sf, sb


def kernel(x_prompt, x_sample, state_gla_fwd, state_gla_bwd, c, c_ctx, norm_mix_g, w_ada, b_ada, w_in, w_a_up_f, b_a_f, w_a_up_b, b_a_b, gla_norm_g, w_pool, pool_scale, w_out, norm_ffn_g, w_router, w_e_gate, w_e_up, w_e_down, final_norm_g):
    depth, d, _ = w_in.shape
    bp, lc, _ = x_prompt.shape
    bs, ls, _ = x_sample.shape
    nq = w_a_up_f.shape[-1]
    dk = nq // GLA_HEADS
    dv = gla_norm_g.shape[-1]
    nv = dv * GLA_HEADS
    rank = w_a_up_f.shape[1]
    nu = pool_scale.shape[-1]
    main_cols = 2 * nq + 2 * nv

    cond = jnp.zeros((_round_up(bs + 1, 2 * SUBLANES_F32), d), F32).at[:bs].set(c).at[bs].set(c_ctx)
    pos = _grid_pos_embed(ls, d, x_sample.dtype)
    tiles_per_seq = ls // TOKEN_TILE
    ctx_row = lambda i: bs
    lat_row = lambda i: i // tiles_per_seq

    xc, xs = x_prompt, x_sample
    new_f, new_b = [], []
    for l in range(depth):
        wi = w_in[l]
        zup = jnp.zeros((rank, nq), F32)
        lp = {
            "gmix": norm_mix_g[l].reshape(1, d),
            "wmain": jnp.concatenate([wi[:, :main_cols], wi[:, main_cols + 2 * rank:]], axis=1).astype(BF16),
            "wza": wi[:, main_cols:main_cols + 2 * rank].astype(BF16),
            "wup": jnp.concatenate([jnp.concatenate([w_a_up_f[l], zup], axis=1),
                                    jnp.concatenate([zup, w_a_up_b[l]], axis=1)], axis=0),
            "bup": jnp.concatenate([b_a_f[l], b_a_b[l]]).reshape(1, 2 * nq),
            "gn": gla_norm_g[l].reshape(1, dv),
            "wpool": w_pool[l].astype(BF16),
            "pscale": pool_scale[l].reshape(1, nu),
            "wout": w_out[l].astype(BF16),
            "gffn": norm_ffn_g[l].reshape(1, d),
            "wrt": w_router[l].T,
            "wg": w_e_gate[l].astype(BF16),
            "wu": w_e_up[l].astype(BF16),
            "wd": w_e_down[l].astype(BF16),
            "fng": final_norm_g.reshape(1, d),
        }
        mod3 = _ada(cond, w_ada[l], b_ada[l]).reshape(cond.shape[0], 6, d)
        last = l == depth - 1

        zero_state = jnp.zeros((bp, GLA_HEADS // 2, 2 * dv, 2 * dk), F32)
        xc, sf, sb = _layer(xc, None, mod3, ctx_row, lp, zero_state, zero_state, lc, last)
        new_f.append(_unpack_state(sf, dk, dv))
        new_b.append(_unpack_state(sb, dk, dv))

        s0f = _pack_state(state_gla_fwd[:, l].astype(F32))
        s0b = _pack_state(state_gla_bwd[:, l].astype(F32))
        xs, _, _ = _layer(xs, pos if l == 0 else None, mod3, lat_row, lp, s0f, s0b, GRID_W, last)

    return (xc, xs, jnp.stack(new_f, axis=1), jnp.stack(new_b, axis=1))
```

```python
import functools

import jax
import jax.numpy as jnp
from jax import lax
from jax.experimental import pallas as pl
from jax.experimental.pallas import tpu as pltpu

F32 = jnp.float32
BF16 = jnp.bfloat16
I32 = jnp.int32

GLA_HEADS = 4
GATE_NORM = 16.0
CHUNK = 64
POOL_WINDOWS = (2, 4, 8, 16)
CAPACITY_FACTOR = 2
GRID_W = 64
EPS = 1e-6

LANES = 128
SUBLANES_F32 = 8
VMEM_LIMIT_BYTES = 56 * 1024 * 1024

TOKEN_TILE = 512
GLA_TILE = 256
ROUTE_BLOCK = 512
ROW_CHUNK = 384
EXPERT_TILE = 512
FF_CHUNK = 256
SLAB_ALIGN = SUBLANES_F32


def _cparams(sem):
    return pltpu.CompilerParams(dimension_semantics=sem, vmem_limit_bytes=VMEM_LIMIT_BYTES)


def _rms(x, g):
    ms = jnp.mean(x * x, axis=-1, keepdims=True)
    return x * lax.rsqrt(ms + EPS) * g


def _silu(x):
    return x * jax.nn.sigmoid(x)


def _ada_kernel(c_ref, w_ref, b_ref, o_ref):
    a = _silu(c_ref[...])
    o_ref[...] = jnp.dot(a, w_ref[...], preferred_element_type=F32) + b_ref[...]


def _ada(cond, w_ada, b_ada):
    rows, d = cond.shape
    n = w_ada.shape[1]
    tn = 1536
    return pl.pallas_call(
        _ada_kernel,
        grid=(n // tn,),
        in_specs=[pl.BlockSpec((rows, d), lambda j: (0, 0)),
                  pl.BlockSpec((d, tn), lambda j: (0, j)),
                  pl.BlockSpec((1, tn), lambda j: (0, j))],
        out_specs=pl.BlockSpec((rows, tn), lambda j: (0, j)),
        out_shape=jax.ShapeDtypeStruct((rows, n), F32),
        compiler_params=_cparams(("parallel",)),
        name="ada",
    )(cond, w_ada, b_ada.reshape(1, n))


def _log_sigmoid(x):
    return jnp.minimum(x, 0.0) - jnp.log1p(jnp.exp(-jnp.abs(x)))


def _in_kernel(has_pos, dk, *refs):
    if has_pos:
        x_ref, pos_ref, *refs = refs
    else:
        x_ref, *refs = refs
    (mod_ref, gmix_ref, wmain_ref, wza_ref, wup_ref, bup_ref,
     q_ref, k_ref, v_ref, g_ref, u_ref, laf_ref, lab_ref) = refs
    x = x_ref[...]
    if has_pos:
        x = x + pos_ref[...]
    h = _rms(x, gmix_ref[...]) * (1.0 + mod_ref[1:2, :]) + mod_ref[0:1, :]
    hb = h.astype(BF16)
    p = jnp.dot(hb, wmain_ref[...], preferred_element_type=F32)
    nq = q_ref.shape[-1]
    nv = v_ref.shape[-1]
    q_ref[...] = (p[:, 0:nq] * (dk ** -0.5)).astype(q_ref.dtype)
    k_ref[...] = p[:, nq:2 * nq].astype(k_ref.dtype)
    v_ref[...] = p[:, 2 * nq:2 * nq + nv].astype(v_ref.dtype)
    g_ref[...] = p[:, 2 * nq + nv:2 * nq + 2 * nv].astype(g_ref.dtype)
    u_ref[...] = p[:, 2 * nq + 2 * nv:].astype(u_ref.dtype)
    za = jnp.dot(hb, wza_ref[...], preferred_element_type=F32)
    z = jnp.dot(za, wup_ref[...], preferred_element_type=F32) + bup_ref[...]
    la = _log_sigmoid(z) * (1.0 / GATE_NORM)
    laf_ref[...] = la[:, 0:nq]
    lab_ref[...] = la[:, nq:]


def _in_proj(x, pos, mod3, mod_row, gmix, wmain, wza, wup, bup, dk, nq, nv, nu):
    n, d = x.shape
    tm = TOKEN_TILE
    tok = lambda i: (i, 0)
    const = lambda i: (0, 0)
    in_specs = [pl.BlockSpec((tm, d), tok)]
    args = [x]
    if pos is not None:
        pos_tiles = pos.shape[0] // tm
        in_specs.append(pl.BlockSpec((tm, d), lambda i: (i % pos_tiles, 0)))
        args.append(pos)
    in_specs += [pl.BlockSpec((None, 6, d), lambda i: (mod_row(i), 0, 0)),
                 pl.BlockSpec((1, d), const),
                 pl.BlockSpec(wmain.shape, const),
                 pl.BlockSpec(wza.shape, const),
                 pl.BlockSpec(wup.shape, const),
                 pl.BlockSpec(bup.shape, const)]
    args += [mod3, gmix, wmain, wza, wup, bup]
    widths = (nq, nq, nv, nv, nu, nq, nq)
    return pl.pallas_call(
        functools.partial(_in_kernel, pos is not None, dk),
        grid=(n // tm,),
        in_specs=in_specs,
        out_specs=[pl.BlockSpec((tm, w), tok) for w in widths],
        out_shape=[jax.ShapeDtypeStruct((n, w), dt) for w, dt in zip(widths, (BF16,) * 5 + (F32,) * 2)],
        compiler_params=_cparams(("parallel",)),
        name="in_proj",
    )(*args)


def _gla_direction(forward, q_ref, k_ref, v_ref, la_ref, st_ref, o_ref):
    t = q_ref.shape[0]
    pair_k = 2 * (q_ref.shape[1] // GLA_HEADS)
    pair_v = 2 * (v_ref.shape[1] // GLA_HEADS)
    nchunk = t // CHUNK
    pos = lax.broadcasted_iota(I32, (t, 1), 0) % CHUNK
    b = la_ref[...]
    s = 1
    while s < CHUNK:
        if forward:
            b = b + jnp.where(pos >= s, pltpu.roll(b, s, axis=0), 0.0)
        else:
            b = b + jnp.where(pos < CHUNK - s, pltpu.roll(b, t - s, axis=0), 0.0)
        s *= 2
    first_k = lax.broadcasted_iota(I32, (CHUNK, pair_k), 1) < pair_k // 2
    qi = lax.broadcasted_iota(I32, (CHUNK, 2 * CHUNK), 0)
    kj = lax.broadcasted_iota(I32, (CHUNK, 2 * CHUNK), 1) % CHUNK
    causal = (kj <= qi) if forward else (kj >= qi)
    st_row = lax.broadcasted_iota(I32, (pair_v, pair_k), 0)
    st_col = lax.broadcasted_iota(I32, (pair_v, pair_k), 1)
    diag = (st_row < pair_v // 2) == (st_col < pair_k // 2)
    order = range(nchunk) if forward else range(nchunk - 1, -1, -1)
    for c in order:
        rows = slice(c * CHUNK, (c + 1) * CHUNK)
        bc = b[rows]
        edge = bc[CHUNK - 1:CHUNK] if forward else bc[0:1]
        qd = q_ref[rows, :].astype(F32) * jnp.exp(bc)
        kc = k_ref[rows, :].astype(F32)
        ki = kc * jnp.exp(-bc)
        ks = kc * jnp.exp(edge - bc)
        dec = jnp.exp(edge)
        vc = v_ref[rows, :]
        for p in range(GLA_HEADS // 2):
            lk = slice(p * pair_k, (p + 1) * pair_k)
            lv = slice(p * pair_v, (p + 1) * pair_v)
            qd_p = qd[:, lk].astype(BF16)
            ki_p = ki[:, lk]
            ks_p = ks[:, lk].astype(BF16)
            v_p = vc[:, lv]
            kbd = jnp.concatenate([jnp.where(first_k, ki_p, 0.0), jnp.where(first_k, 0.0, ki_p)], axis=0).astype(BF16)
            attn = lax.dot_general(qd_p, kbd, (((1,), (1,)), ((), ())), preferred_element_type=F32)
            attn = jnp.where(causal, attn, 0.0).astype(BF16)
            zv = jnp.zeros((CHUNK, pair_v // 2), v_p.dtype)
            vbd = jnp.concatenate([jnp.concatenate([v_p[:, :pair_v // 2], zv], axis=1),
                                   jnp.concatenate([zv, v_p[:, pair_v // 2:]], axis=1)], axis=0)
            st = st_ref[p]
            o = jnp.dot(attn, vbd, preferred_element_type=F32)
            o = o + lax.dot_general(qd_p, st.astype(BF16), (((1,), (1,)), ((), ())), preferred_element_type=F32)
            o_ref[rows, lv] = o.astype(o_ref.dtype)
            kv = lax.dot_general(v_p, ks_p, (((0,), (0,)), ((), ())), preferred_element_type=F32)
            st_ref[p] = st * dec[:, lk] + jnp.where(diag, kv, 0.0)


def _gla_kernel(qf, kf, vf, laf, qb, kb, vb, lab, s0f, s0b, of, ob, sf, sb, stf, stb):
    i = pl.program_id(1)

    @pl.when(i == 0)
    def _():
        stf[...] = s0f[...]
        stb[...] = s0b[...]

    _gla_direction(True, qf, kf, vf, laf, stf, of)
    _gla_direction(False, qb, kb, vb, lab, stb, ob)

    @pl.when(i == pl.num_programs(1) - 1)
    def _():
        sf[...] = stf[...]
        sb[...] = stb[...]


def _gla(q, k, v, laf, lab, s0f, s0b):
    bsz, l, nq = q.shape
    nv = v.shape[-1]
    t = GLA_TILE
    ns = l // t
    fwd = lambda b, i: (b, i, 0)
    bwd = lambda b, i: (b, ns - 1 - i, 0)
    st = lambda b, i: (b, 0, 0, 0)
    st_block = (None,) + s0f.shape[1:]

    def seq(width, imap):
        return pl.BlockSpec((None, t, width), imap)

    return pl.pallas_call(
        _gla_kernel,
        grid=(bsz, ns),
        in_specs=[seq(nq, fwd), seq(nq, fwd), seq(nv, fwd), seq(nq, fwd),
                  seq(nq, bwd), seq(nq, bwd), seq(nv, bwd), seq(nq, bwd),
                  pl.BlockSpec(st_block, st), pl.BlockSpec(st_block, st)],
        out_specs=[seq(nv, fwd), seq(nv, bwd), pl.BlockSpec(st_block, st), pl.BlockSpec(st_block, st)],
        out_shape=[jax.ShapeDtypeStruct((bsz, l, nv), BF16), jax.ShapeDtypeStruct((bsz, l, nv), BF16),
                   jax.ShapeDtypeStruct(s0f.shape, F32), jax.ShapeDtypeStruct(s0b.shape, F32)],
        scratch_shapes=[pltpu.VMEM(s0f.shape[1:], F32), pltpu.VMEM(s0b.shape[1:], F32)],
        compiler_params=_cparams(("parallel", "arbitrary")),
        name="gla",
    )(q, k, v, laf, q, k, v, lab, s0f, s0b)


def _pack_state(s):
    bsz, h, dk, dv = s.shape
    st = jnp.swapaxes(s, -1, -2).reshape(bsz, h // 2, 2, dv, dk)
    z = jnp.zeros_like(st[:, :, 0])
    top = jnp.concatenate([st[:, :, 0], z], axis=-1)
    bot = jnp.concatenate([z, st[:, :, 1]], axis=-1)
    return jnp.concatenate([top, bot], axis=-2)


def _unpack_state(sp, dk, dv):
    a = sp[:, :, :dv, :dk]
    b = sp[:, :, dv:, dk:]
    st = jnp.stack([a, b], axis=2)
    bsz, hp = sp.shape[:2]
    return jnp.swapaxes(st.reshape(bsz, 2 * hp, dv, dk), -1, -2)


def _mix_kernel(has_pos, seg_len, *refs):
    if has_pos:
        x_ref, pos_ref, *refs = refs
    else:
        x_ref, *refs = refs
    (of_ref, ob_ref, g_ref, u_ref, mod_ref, gn_ref, wpool_ref, pscale_ref, wout_ref, gffn_ref, wrt_ref,
     x1_ref, h2_ref, probs_ref) = refs
    tm = x_ref.shape[0]
    x = x_ref[...]
    if has_pos:
        x = x + pos_ref[...]
    dv = gn_ref.shape[-1]
    nv = of_ref.shape[-1]
    o = of_ref[...].astype(F32) + ob_ref[...].astype(F32)
    gn = gn_ref[...]
    y = jnp.zeros((tm, x.shape[1]), F32)
    for h in range(nv // dv):
        cols = slice(h * dv, (h + 1) * dv)
        oh = _rms(o[:, cols], gn) * _silu(g_ref[:, cols].astype(F32))
        y = y + jnp.dot(oh.astype(BF16), wout_ref[cols, :], preferred_element_type=F32)
    pos = lax.broadcasted_iota(I32, (tm, 1), 0) % seg_len
    gc = u_ref.shape[-1] // len(POOL_WINDOWS)
    for gi, w in enumerate(POOL_WINDOWS):
        cols = slice(gi * gc, (gi + 1) * gc)
        u = u_ref[:, cols].astype(F32)
        half = w // 2
        left = jnp.where(pos >= 1, pltpu.roll(u, 1, axis=0), 0.0)
        m = 1
        while m < half:
            left = left + jnp.where(pos >= m, pltpu.roll(left, m, axis=0), 0.0)
            m *= 2
        right = u
        m = 1
        while m < half:
            right = right + jnp.where(pos + m < seg_len, pltpu.roll(right, tm - m, axis=0), 0.0)
            m *= 2
        lo = jnp.maximum(pos - half, 0)
        hi = jnp.minimum(pos - half + w - 1, seg_len - 1)
        cnt = (hi - lo + 1).astype(F32)
        pooled = (left + right) / cnt - u
        po = jnp.dot(pooled.astype(BF16), wpool_ref[gi], preferred_element_type=F32) * pscale_ref[:, cols]
        y = y + jnp.dot(po.astype(BF16), wout_ref[nv + gi * gc:nv + (gi + 1) * gc, :], preferred_element_type=F32)
    x1 = x + mod_ref[2:3, :] * y
    x1_ref[...] = x1
    h2 = _rms(x1, gffn_ref[...]) * (1.0 + mod_ref[4:5, :]) + mod_ref[3:4, :]
    h2_ref[...] = h2.astype(BF16)
    logits = lax.dot_general(wrt_ref[...], h2, (((1,), (1,)), ((), ())), preferred_element_type=F32)
    mx = jnp.max(logits, axis=0, keepdims=True)
    ex = jnp.exp(logits - mx)
    probs_ref[...] = ex / jnp.sum(ex, axis=0, keepdims=True)


def _mix(x, pos, of, ob, g, u, mod3, mod_row, gn, wpool, pscale, wout, gffn, wrt, seg_len):
    n, d = x.shape
    tm = TOKEN_TILE
    ne = wrt.shape[0]
    tok = lambda i: (i, 0)
    const2 = lambda i: (0, 0)
    in_specs = [pl.BlockSpec((tm, d), tok)]
    args = [x]
    if pos is not None:
        pos_tiles = pos.shape[0] // tm
        in_specs.append(pl.BlockSpec((tm, d), lambda i: (i % pos_tiles, 0)))
        args.append(pos)
    in_specs += [pl.BlockSpec((tm, of.shape[1]), tok), pl.BlockSpec((tm, ob.shape[1]), tok),
                 pl.BlockSpec((tm, g.shape[1]), tok), pl.BlockSpec((tm, u.shape[1]), tok),
                 pl.BlockSpec((None, 6, d), lambda i: (mod_row(i), 0, 0)),
                 pl.BlockSpec(gn.shape, const2),
                 pl.BlockSpec(wpool.shape, lambda i: (0, 0, 0)),
                 pl.BlockSpec(pscale.shape, const2),
                 pl.BlockSpec(wout.shape, const2),
                 pl.BlockSpec(gffn.shape, const2),
                 pl.BlockSpec(wrt.shape, const2)]
    args += [of, ob, g, u, mod3, gn, wpool, pscale, wout, gffn, wrt]
    return pl.pallas_call(
        functools.partial(_mix_kernel, pos is not None, seg_len),
        grid=(n // tm,),
        in_specs=in_specs,
        out_specs=[pl.BlockSpec((tm, d), tok), pl.BlockSpec((tm, d), tok), pl.BlockSpec((ne, tm), lambda i: (0, i))],
        out_shape=[jax.ShapeDtypeStruct((n, d), F32), jax.ShapeDtypeStruct((n, d), BF16),
                   jax.ShapeDtypeStruct((ne, n), F32)],
        compiler_params=_cparams(("parallel",)),
        name="mix",
    )(*args)


def _topk_kernel(cap, probs_ref, gate_ref, possel_ref, posall_ref, thr_ref):
    ne, nr, _ = probs_ref.shape
    probs = probs_ref[...]

    def bit_step(i, thr):
        cand = thr | (jnp.int32(1) << (30 - i))
        cnt = jnp.sum(jnp.where(probs >= pltpu.bitcast(cand, F32), 1.0, 0.0), axis=(1, 2), keepdims=True)
        return jnp.where(cnt >= cap, cand, thr)

    thr = lax.fori_loop(0, 31, bit_step, jnp.zeros((ne, 1, 1), I32))
    thr_ref[...] = jnp.broadcast_to(pltpu.bitcast(thr, F32), thr_ref.shape)

    li = lax.broadcasted_iota(I32, (LANES, LANES), 0)
    lj = lax.broadcasted_iota(I32, (LANES, LANES), 1)
    before_lane = jnp.where(li < lj, 1.0, 0.0).astype(BF16)
    ri = lax.broadcasted_iota(I32, (nr, nr), 0)
    rj = lax.broadcasted_iota(I32, (nr, nr), 1)
    before_row = jnp.where(rj < ri, 1.0, 0.0).astype(BF16)

    def excl_prefix(m):
        mb = m.astype(BF16)
        in_row = jnp.dot(mb, before_lane, preferred_element_type=F32)
        rows_before = jnp.sum(jnp.dot(before_row, mb, preferred_element_type=F32), axis=1, keepdims=True)
        return in_row + rows_before

    def per_expert(e, carry):
        p = probs_ref[e]
        at_least = p >= thr_ref[e][0:1, :]
        kth = jnp.min(jnp.where(at_least, p, jnp.inf), axis=(0, 1), keepdims=True)
        gt = p > kth
        eq = p == kth
        n_gt = jnp.sum(jnp.where(gt, 1.0, 0.0), axis=(0, 1), keepdims=True)
        need = cap - n_gt
        eq_rank = excl_prefix(jnp.where(eq, 1.0, 0.0))
        sel = gt | (eq & (eq_rank < need))
        pos = excl_prefix(jnp.where(sel, 1.0, 0.0)).astype(I32)
        gate_ref[e] = jnp.where(sel, p, 0.0)
        possel_ref[e] = jnp.where(sel, pos, -1)
        posall_ref[e] = pos
        return carry

    lax.fori_loop(0, ne, per_expert, 0)


def _topk(probs3, cap):
    ne, nr, _ = probs3.shape
    full = lambda: (0, 0, 0)
    spec = pl.BlockSpec(probs3.shape, full)
    return pl.pallas_call(
        functools.partial(_topk_kernel, cap),
        in_specs=[spec],
        out_specs=[spec, spec, spec],
        out_shape=[jax.ShapeDtypeStruct(probs3.shape, F32), jax.ShapeDtypeStruct(probs3.shape, I32),
                   jax.ShapeDtypeStruct(probs3.shape, I32)],
        scratch_shapes=[pltpu.VMEM((ne, SUBLANES_F32, LANES), F32)],
        compiler_params=pltpu.CompilerParams(vmem_limit_bytes=VMEM_LIMIT_BYTES),
        name="topk",
    )(probs3)


def _num_chunks(meta, b):
    return (meta[4][b] + ROW_CHUNK - 1) // ROW_CHUNK


def _expert_range(meta, b, nb, k):
    kmax = meta[5].shape[0] // nb
    return meta[5][b * kmax + k], meta[6][b * kmax + k]


def _for_segments(meta, b, nb, k, make_copy, action):
    _, ro_ref, len_ref, off_ref = meta[:4]

    def body(e, carry):
        r0 = ro_ref[e * nb + b]
        g0 = jnp.maximum(r0, k * ROW_CHUNK)
        g1 = jnp.minimum(r0 + len_ref[e * nb + b], (k + 1) * ROW_CHUNK)
        cnt = pl.multiple_of(jnp.maximum(g1 - g0, 0), SLAB_ALIGN)
        chunk_row = pl.multiple_of(g0 - k * ROW_CHUNK, SLAB_ALIGN)
        region_row = pl.multiple_of(off_ref[e * nb + b] + g0 - r0, SLAB_ALIGN)

        @pl.when(cnt > 0)
        def _():
            action(make_copy(e, cnt, chunk_row, region_row))
        return carry

    lax.fori_loop(*_expert_range(meta, b, nb, k), body, 0)


def _build_rows(meta, b, nb, k, possel_ref, value_ref, dst):
    base_ref, ro_ref, len_ref, _, tot_ref = meta[:5]
    s = dst.shape[1]
    c0 = k * ROW_CHUNK
    sub = lax.broadcasted_iota(I32, (SLAB_ALIGN, s), 0)

    def per_expert(e, carry):
        r0 = ro_ref[e * nb + b]
        lo = jnp.maximum(r0, c0)
        hi = jnp.minimum(r0 + len_ref[e * nb + b], c0 + ROW_CHUNK)
        ps = possel_ref[pl.ds(e, 1), :]
        rel = jnp.where(ps >= 0, ps - (base_ref[e * nb + b] - r0 + c0), -1)
        val = 1.0 if value_ref is None else value_ref[pl.ds(e, 1), :]

        def group(j, c):
            row = pl.multiple_of(lo - c0 + j * SLAB_ALIGN, SLAB_ALIGN)
            dst[pl.ds(row, SLAB_ALIGN), :] = jnp.where((rel - row) == sub, val, 0.0)
            return c

        lax.fori_loop(0, jnp.maximum(hi - lo, 0) // SLAB_ALIGN, group, 0)
        return carry

    lax.fori_loop(*_expert_range(meta, b, nb, k), per_expert, 0)
    end = jnp.clip(tot_ref[b] - c0, 0, ROW_CHUNK)

    def clear(j, c):
        row = pl.multiple_of(end + j * SLAB_ALIGN, SLAB_ALIGN)
        dst[pl.ds(row, SLAB_ALIGN), :] = jnp.zeros((SLAB_ALIGN, s), F32)
        return c

    lax.fori_loop(0, (ROW_CHUNK - end) // SLAB_ALIGN, clear, 0)


def _gather_kernel(ne, nbc, nb, *refs):
    meta = refs[:7]
    hc_ref, hl_ref, possel_ref, xe_hbm, buf, onehot, sem = refs[7:]
    len_ref, off_ref = meta[2], meta[3]
    b = pl.program_id(0)
    nch = _num_chunks(meta, b)

    def copies(blk, k, slot, action):
        def make_copy(e, cnt, chunk_row, region_row):
            return pltpu.make_async_copy(buf.at[slot, pl.ds(chunk_row, cnt)],
                                         xe_hbm.at[e, pl.ds(region_row, cnt)], sem.at[slot])
        _for_segments(meta, blk, nb, k, make_copy, action)

    def drain(blk):
        for back in (2, 1):
            k = _num_chunks(meta, blk) - back

            @pl.when(k >= 0)
            def _():
                copies(blk, k, k % 2, lambda cp: cp.wait())

    prev = jnp.maximum(b - 1, 0)

    def chunk(k, carry):
        slot = k % 2
        _build_rows(meta, b, nb, k, possel_ref, None, onehot)

        @pl.when(k >= 2)
        def _():
            copies(b, k - 2, slot, lambda cp: cp.wait())

        @pl.when((k == 0) & (b > 0))
        def _():
            drain(prev)

        @pl.when(b < nbc)
        def _():
            buf[slot] = jnp.dot(onehot[...].astype(BF16), hc_ref[...], preferred_element_type=F32)

        @pl.when(b >= nbc)
        def _():
            buf[slot] = jnp.dot(onehot[...].astype(BF16), hl_ref[...], preferred_element_type=F32)

        copies(b, k, slot, lambda cp: cp.start())
        return carry

    lax.fori_loop(0, nch, chunk, 0)

    @pl.when((nch == 0) & (b > 0))
    def _():
        drain(prev)

    @pl.when(b == nb - 1)
    def _():
        drain(b)
        capp = xe_hbm.shape[1]
        buf[0] = jnp.zeros(buf.shape[1:], F32)

        def tail(action):
            def per_expert(e, carry):
                used = off_ref[e * nb + nb - 1] + len_ref[e * nb + nb - 1]

                def piece(j, c):
                    start = pl.multiple_of(used + j * ROW_CHUNK, SLAB_ALIGN)
                    cnt = pl.multiple_of(jnp.minimum(ROW_CHUNK, capp - start), SLAB_ALIGN)
                    action(pltpu.make_async_copy(buf.at[0, pl.ds(0, cnt)], xe_hbm.at[e, pl.ds(start, cnt)],
                                                 sem.at[0]))
                    return c

                lax.fori_loop(0, (capp - used + ROW_CHUNK - 1) // ROW_CHUNK, piece, 0)
                return carry

            lax.fori_loop(0, ne, per_expert, 0)

        tail(lambda cp: cp.start())
        tail(lambda cp: cp.wait())


def _gather(meta, h2c, h2l, possel2, capp):
    d = h2c.shape[1]
    ne = possel2.shape[0]
    s = ROUTE_BLOCK
    nbc = h2c.shape[0] // s
    nb = nbc + h2l.shape[0] // s
    grid_spec = pltpu.PrefetchScalarGridSpec(
        num_scalar_prefetch=len(meta),
        grid=(nb,),
        in_specs=[pl.BlockSpec((s, d), lambda b, *_: (jnp.minimum(b, nbc - 1), 0)),
                  pl.BlockSpec((s, d), lambda b, *_: (jnp.maximum(b - nbc, 0), 0)),
                  pl.BlockSpec((ne, s), lambda b, *_: (0, b))],
        out_specs=pl.BlockSpec(memory_space=pl.ANY),
        scratch_shapes=[pltpu.VMEM((2, ROW_CHUNK, d), F32), pltpu.VMEM((ROW_CHUNK, s), F32),
                        pltpu.SemaphoreType.DMA((2,))],
    )
    return pl.pallas_call(
        functools.partial(_gather_kernel, ne, nbc, nb),
        grid_spec=grid_spec,
        out_shape=jax.ShapeDtypeStruct((ne, capp, d), F32),
        compiler_params=_cparams(("arbitrary",)),
        name="gather",
    )(*meta, h2c, h2l, possel2)


def _expert_kernel(used_ref, xe_ref, wg_ref, wu_ref, wd_ref, ye_ref):
    e = pl.program_id(0)
    m = pl.program_id(1)
    tm, d = xe_ref.shape
    f = wg_ref.shape[-1]
    valid = used_ref[e] - m * tm

    @pl.when(valid > 0)
    def _():
        x = xe_ref[...].astype(BF16)
        acc = jnp.zeros((tm, d), F32)
        for j in range(f // FF_CHUNK):
            cols = slice(j * FF_CHUNK, (j + 1) * FF_CHUNK)
            hg = jnp.dot(x, wg_ref[:, cols], preferred_element_type=F32)
            hu = jnp.dot(x, wu_ref[:, cols], preferred_element_type=F32)
            hid = (_silu(hg) * hu).astype(BF16)
            acc = acc + jnp.dot(hid, wd_ref[cols, :], preferred_element_type=F32)
        ye_ref[...] = acc

    @pl.when(valid <= 0)
    def _():
        ye_ref[...] = jnp.zeros(ye_ref.shape, F32)


def _experts(used, xe, wg, wu, wd):
    ne, capp, d = xe.shape
    f = wg.shape[-1]
    tm = EXPERT_TILE
    grid_spec = pltpu.PrefetchScalarGridSpec(
        num_scalar_prefetch=1,
        grid=(ne, capp // tm),
        in_specs=[pl.BlockSpec((None, tm, d), lambda e, m, *_: (e, m, 0)),
                  pl.BlockSpec((None, d, f), lambda e, m, *_: (e, 0, 0)),
                  pl.BlockSpec((None, d, f), lambda e, m, *_: (e, 0, 0)),
                  pl.BlockSpec((None, f, d), lambda e, m, *_: (e, 0, 0))],
        out_specs=pl.BlockSpec((None, tm, d), lambda e, m, *_: (e, m, 0)),
    )
    return pl.pallas_call(
        _expert_kernel,
        grid_spec=grid_spec,
        out_shape=jax.ShapeDtypeStruct((ne, capp, d), F32),
        compiler_params=_cparams(("parallel", "arbitrary")),
        name="experts",
    )(used, xe, wg, wu, wd)


def _combine_kernel(blk0, nb, final_norm, *refs):
    meta = refs[:7]
    ye_hbm, possel_ref, gate_ref, x1_ref, mod_ref, fng_ref, o_ref, buf, weights, acc, sem = refs[7:]
    i = pl.program_id(0)
    b = blk0 + i
    nch = _num_chunks(meta, b)

    def copies(blk, k, slot, action):
        def make_copy(e, cnt, chunk_row, region_row):
            return pltpu.make_async_copy(ye_hbm.at[e, pl.ds(region_row, cnt)],
                                         buf.at[slot, pl.ds(chunk_row, cnt)], sem.at[slot])
        _for_segments(meta, blk, nb, k, make_copy, action)

    @pl.when(i == 0)
    def _():
        buf[...] = jnp.zeros(buf.shape, F32)

        @pl.when(nch > 0)
        def _():
            copies(b, 0, 0, lambda cp: cp.start())

    acc[...] = jnp.zeros(acc.shape, F32)

    def chunk(k, carry):
        slot = k % 2

        @pl.when(k + 1 < nch)
        def _():
            copies(b, k + 1, 1 - slot, lambda cp: cp.start())

        _build_rows(meta, b, nb, k, possel_ref, gate_ref, weights)
        copies(b, k, slot, lambda cp: cp.wait())
        acc[...] += lax.dot_general(weights[...].astype(BF16), buf[slot].astype(BF16), (((0,), (0,)), ((), ())),
                                    preferred_element_type=F32)
        return carry

    lax.fori_loop(0, nch, chunk, 0)

    last = i == pl.num_programs(0) - 1
    nxt = jnp.where(last, b, b + 1)

    @pl.when(jnp.logical_not(last) & (_num_chunks(meta, nxt) > 0))
    def _():
        copies(nxt, 0, 0, lambda cp: cp.start())

    x2 = x1_ref[...] + mod_ref[5:6, :] * acc[...]
    o_ref[...] = _rms(x2, fng_ref[...]) if final_norm else x2


def _combine(meta, blk0, nb, ye, possel2, gate2, x1, mod3, mod_row, fng, final_norm):
    n, d = x1.shape
    ne = ye.shape[0]
    s = ROUTE_BLOCK
    per_block = s // TOKEN_TILE
    grid_spec = pltpu.PrefetchScalarGridSpec(
        num_scalar_prefetch=len(meta),
        grid=(n // s,),
        in_specs=[pl.BlockSpec(memory_space=pl.ANY),
                  pl.BlockSpec((ne, s), lambda i, *_: (0, blk0 + i)),
                  pl.BlockSpec((ne, s), lambda i, *_: (0, blk0 + i)),
                  pl.BlockSpec((s, d), lambda i, *_: (i, 0)),
                  pl.BlockSpec((None, 6, d), lambda i, *_: (mod_row(i * per_block), 0, 0)),
                  pl.BlockSpec((1, d), lambda i, *_: (0, 0))],
        out_specs=pl.BlockSpec((s, d), lambda i, *_: (i, 0)),
        scratch_shapes=[pltpu.VMEM((2, ROW_CHUNK, d), F32), pltpu.VMEM((ROW_CHUNK, s), F32),
                        pltpu.VMEM((s, d), F32), pltpu.SemaphoreType.DMA((2,))],
    )
    return pl.pallas_call(
        functools.partial(_combine_kernel, blk0, nb, final_norm),
        grid_spec=grid_spec,
        out_shape=jax.ShapeDtypeStruct((n, d), F32),
        compiler_params=_cparams(("arbitrary",)),
        name="combine",
    )(*meta, ye, possel2, gate2, x1, mod3, fng)


def _round_up(x, m):
    return (x + m - 1) // m * m


def _route(probs_t):
    ne, n = probs_t.shape
    cap = CAPACITY_FACTOR * n // ne
    gate3, possel3, posall3 = _topk(probs_t.reshape(ne, n // LANES, LANES), cap)
    base = posall3[:, ::ROUTE_BLOCK // LANES, 0]
    count = jnp.concatenate([base[:, 1:], jnp.full((ne, 1), cap, I32)], axis=1) - base
    return gate3.reshape(ne, n), possel3.reshape(ne, n), base, count, cap


def _moe(groups, wg, wu, wd, fng, final_norm):
    (x1c, h2c, probs_c, _, _), (x1l, h2l, probs_l, _, _) = groups
    ne = probs_c.shape[0]
    routed = [_route(g[2]) for g in groups]
    gate2 = jnp.concatenate([r[0] for r in routed], axis=1)
    possel2 = jnp.concatenate([r[1] for r in routed], axis=1)
    base = jnp.concatenate([r[2] for r in routed], axis=1)
    count = jnp.concatenate([r[3] for r in routed], axis=1)
    nb = base.shape[1]
    slab = (count + SLAB_ALIGN - 1) // SLAB_ALIGN * SLAB_ALIGN
    off = jnp.cumsum(slab, axis=1) - slab
    ro = jnp.cumsum(slab, axis=0) - slab
    tot = jnp.sum(slab, axis=0)
    used = jnp.sum(slab, axis=1)
    kmax = -(-ne * ROUTE_BLOCK // ROW_CHUNK)
    edge = (jnp.arange(kmax, dtype=I32) * ROW_CHUNK)[None, :, None]
    elo = jnp.sum(((ro + slab).T[:, None, :] <= edge).astype(I32), axis=-1)
    ehi = jnp.sum((ro.T[:, None, :] < edge + ROW_CHUNK).astype(I32), axis=-1)
    meta = (base.reshape(-1), ro.reshape(-1), slab.reshape(-1), off.reshape(-1), tot, elo.reshape(-1), ehi.reshape(-1))
    capp = _round_up(sum(r[4] for r in routed) + (SLAB_ALIGN - 1) * nb, EXPERT_TILE)

    xe = _gather(meta, h2c, h2l, possel2, capp)
    ye = _experts(used, xe, wg, wu, wd)
    outs = []
    blk0 = 0
    for x1, _, _, mod3, mod_row in groups:
        outs.append(_combine(meta, blk0, nb, ye, possel2, gate2, x1, mod3, mod_row, fng, final_norm))
        blk0 += x1.shape[0] // ROUTE_BLOCK
    return outs


def _grid_pos_embed(length, d, dtype):
    rows = length // GRID_W
    row = jnp.repeat(jnp.arange(rows), GRID_W).astype(F32)
    col = jnp.tile(jnp.arange(GRID_W), rows).astype(F32)
    quarter = d // 4
    omega = 1.0 / (10000.0 ** (jnp.arange(quarter, dtype=F32) / quarter))
    er = row[:, None] * omega[None, :]
    ec = col[:, None] * omega[None, :]
    return jnp.concatenate([jnp.sin(er), jnp.cos(er), jnp.sin(ec), jnp.cos(ec)], axis=-1).astype(dtype)


def _mixer(x, pos, mod3, mod_row, lp, s0f, s0b, seg_len):
    bsz, l, d = x.shape
    n = bsz * l
    xf = x.reshape(n, d)
    nq = lp["wup"].shape[1] // 2
    nv = lp["gn"].shape[1] * GLA_HEADS
    nu = lp["pscale"].shape[1]
    dk = nq // GLA_HEADS
    q, k, v, g, u, laf, lab = _in_proj(xf, pos, mod3, mod_row, lp["gmix"], lp["wmain"], lp["wza"], lp["wup"],
                                       lp["bup"], dk, nq, nv, nu)
    seq = lambda a: a.reshape(bsz, l, a.shape[-1])
    of, ob, sf, sb = _gla(seq(q), seq(k), seq(v), seq(laf), seq(lab), s0f, s0b)
    x1, h2, probs_t = _mix(xf, pos, of.reshape(n, nv), ob.reshape(n, nv), g, u, mod3, mod_row, lp["gn"], lp["wpool"],
                           lp["pscale"], lp["wout"], lp["gffn"], lp["wrt"], seg_len)
    return (x1, h2, probs_t, mod3, mod_row), sf, sb


def kernel(x_prompt, x_sample, state_gla_fwd, state_gla_bwd, c, c_ctx, norm_mix_g, w_ada, b_ada, w_in, w_a_up_f, b_a_f, w_a_up_b, b_a_b, gla_norm_g, w_pool, pool_scale, w_out, norm_ffn_g, w_router, w_e_gate, w_e_up, w_e_down, final_norm_g):
    depth, d, _ = w_in.shape
    bp, lc, _ = x_prompt.shape
    bs, ls, _ = x_sample.shape
    nq = w_a_up_f.shape[-1]
    dk = nq // GLA_HEADS
    dv = gla_norm_g.shape[-1]
    nv = dv * GLA_HEADS
    rank = w_a_up_f.shape[1]
    nu = pool_scale.shape[-1]
    main_cols = 2 * nq + 2 * nv

    cond = jnp.zeros((_round_up(bs + 1, 2 * SUBLANES_F32), d), F32).at[:bs].set(c).at[bs].set(c_ctx)
    pos = _grid_pos_embed(ls, d, x_sample.dtype)
    tiles_per_seq = ls // TOKEN_TILE
    ctx_row = lambda i: bs
    lat_row = lambda i: i // tiles_per_seq

    xc, xs = x_prompt, x_sample
    new_f, new_b = [], []
    for l in range(depth):
        wi = w_in[l]
        zup = jnp.zeros((rank, nq), F32)
        lp = {
            "gmix": norm_mix_g[l].reshape(1, d),
            "wmain": jnp.concatenate([wi[:, :main_cols], wi[:, main_cols + 2 * rank:]], axis=1).astype(BF16),
            "wza": wi[:, main_cols:main_cols + 2 * rank].astype(BF16),
            "wup": jnp.concatenate([jnp.concatenate([w_a_up_f[l], zup], axis=1),
                                    jnp.concatenate([zup, w_a_up_b[l]], axis=1)], axis=0),
            "bup": jnp.concatenate([b_a_f[l], b_a_b[l]]).reshape(1, 2 * nq),
            "gn": gla_norm_g[l].reshape(1, dv),
            "wpool": w_pool[l].astype(BF16),
            "pscale": pool_scale[l].reshape(1, nu),
            "wout": w_out[l].astype(BF16),
            "gffn": norm_ffn_g[l].reshape(1, d),
            "wrt": w_router[l].T,
            "wg": w_e_gate[l].astype(BF16),
            "wu": w_e_up[l].astype(BF16),
            "wd": w_e_down[l].astype(BF16),
            "fng": final_norm_g.reshape(1, d),
        }
        mod3 = _ada(cond, w_ada[l], b_ada[l]).reshape(cond.shape[0], 6, d)
        last = l == depth - 1

        zero_state = jnp.zeros((bp, GLA_HEADS // 2, 2 * dv, 2 * dk), F32)
        ctx, sf, sb = _mixer(xc, None, mod3, ctx_row, lp, zero_state, zero_state, lc)
        new_f.append(_unpack_state(sf, dk, dv))
        new_b.append(_unpack_state(sb, dk, dv))

        s0f = _pack_state(state_gla_fwd[:, l].astype(F32))
        s0b = _pack_state(state_gla_bwd[:, l].astype(F32))
        lat, _, _ = _mixer(xs, pos if l == 0 else None, mod3, lat_row, lp, s0f, s0b, GRID_W)

        yc, ys = _moe((ctx, lat), lp["wg"], lp["wu"], lp["wd"], lp["fng"], last)
        xc, xs = yc.reshape(xc.shape), ys.reshape(xs.shape)

    return (xc, xs, jnp.stack(new_f, axis=1), jnp.stack(new_b, axis=1))
```
